```python
import math
import jax, jax.numpy as jnp
from jax import lax
import numpy as np

D_MODEL = 1024
BATCH = 2
SEQ = 8192
DEPTH = 2

HEAD_DIM = 64
N_A_LAYERS = DEPTH // 2
N_B_LAYERS = DEPTH - N_A_LAYERS
MAIN_W = (3 * D_MODEL) // 4
DIFF_HEADS = MAIN_W // (2 * HEAD_DIM)
DIFF_W = DIFF_HEADS * 2 * HEAD_DIM
SB_HEADS = MAIN_W // HEAD_DIM
SB_W = SB_HEADS * HEAD_DIM
MEM_HEADS = 4
MEM_LEN = 256
MEM_W = D_MODEL - MAIN_W
MEM_HEAD_DIM = MEM_W // MEM_HEADS
MIX_W = DIFF_W + MEM_W
A_IN = 3 * DIFF_W + MEM_W + MIX_W
B_IN = SB_W + MEM_W + MIX_W
ROPE_DIM = HEAD_DIM // 4
ROPE_THETA = 500000.0
BLOCK = 128
EPS = 1e-6

kernel_name = "yoco_diffattn_stickbreak_memory_hybrid"


def rms_norm(x, g):
    xf = x.astype(jnp.float32)
    y = xf * lax.rsqrt(jnp.mean(xf * xf, axis=-1, keepdims=True) + EPS)
    return (y * g.astype(jnp.float32)).astype(x.dtype)


def rope_cos_sin(positions):
    inv = ROPE_THETA ** (-jnp.arange(0, ROPE_DIM, 2, dtype=jnp.float32) / ROPE_DIM)
    ang = positions.astype(jnp.float32)[..., None] * inv
    return jnp.cos(ang), jnp.sin(ang)


def apply_partial_rope(x, cos, sin):
    half = ROPE_DIM // 2
    c = cos[:, :, None, None, :].astype(x.dtype)
    s = sin[:, :, None, None, :].astype(x.dtype)
    x1 = x[..., :half]
    x2 = x[..., half:ROPE_DIM]
    return jnp.concatenate([x1 * c - x2 * s, x2 * c + x1 * s, x[..., ROPE_DIM:]], axis=-1)


def diff_attention(q, k, v, lam):
    S = q.shape[3]
    scale = HEAD_DIM ** -0.5
    outs = []
    for i in range(S // BLOCK):
        s0, s1 = i * BLOCK, (i + 1) * BLOCK
        sc = jnp.einsum('bchqd,bchkd->bchqk', q[:, :, :, s0:s1], k[:, :, :, :s1]).astype(jnp.float32) * scale
        mask = (s0 + jnp.arange(BLOCK))[:, None] >= jnp.arange(s1)[None, :]
        p = jax.nn.softmax(jnp.where(mask, sc, -jnp.inf), axis=-1)
        w = p[:, 0] - lam * p[:, 1]
        outs.append(jnp.einsum('bhqk,bhkd->bhqd', w.astype(v.dtype), v[:, :, :s1]))
    return jnp.concatenate(outs, axis=2)


def stick_breaking_attention(q, k, v):
    S = q.shape[2]
    scale = HEAD_DIM ** -0.5
    outs = []
    for i in range(S // BLOCK):
        s0, s1 = i * BLOCK, (i + 1) * BLOCK
        z = jnp.einsum('bhqd,bhkd->bhqk', q[:, :, s0:s1], k[:, :, :s1]).astype(jnp.float32) * scale
        mask = (s0 + jnp.arange(BLOCK))[:, None] > jnp.arange(s1)[None, :]
        log_beta = jax.nn.log_sigmoid(z)
        log_1m = jnp.where(mask, jax.nn.log_sigmoid(-z), 0.0)
        suffix = lax.cumsum(log_1m, axis=3, reverse=True) - log_1m
        a = jnp.exp(jnp.where(mask, log_beta + suffix, -jnp.inf))
        outs.append(jnp.einsum('bhqk,bhkd->bhqd', a.astype(v.dtype), v[:, :, :s1]))
    return jnp.concatenate(outs, axis=2)


def memory_attention(qm, mem, l, mem_norm, mem_w_kv, mem_q_norm, mem_k_norm):
    B, S, _ = qm.shape
    M = mem.shape[1]
    kvm = rms_norm(mem, mem_norm[l]) @ mem_w_kv[l]
    km, vm = jnp.split(kvm, 2, axis=-1)
    km = rms_norm(km.reshape(B, M, MEM_HEADS, MEM_HEAD_DIM), mem_k_norm[l])
    vm = vm.reshape(B, M, MEM_HEADS, MEM_HEAD_DIM)
    qh = rms_norm(qm.reshape(B, S, MEM_HEADS, MEM_HEAD_DIM), mem_q_norm[l])
    sc = jnp.einsum('bshd,bmhd->bhsm', qh, km).astype(jnp.float32) * (MEM_HEAD_DIM ** -0.5)
    p = jax.nn.softmax(sc, axis=-1)
    o = jnp.einsum('bhsm,bmhd->bshd', p.astype(vm.dtype), vm)
    return o.reshape(B, S, MEM_W)


def setup_inputs(seed: int = 0) -> dict:
    key = jax.random.key(seed)
    ks = jax.random.split(key, 24)
    nrm = lambda k, shape: jax.random.normal(k, shape, jnp.float32)
    gain = lambda k, shape: 1.0 + 0.02 * nrm(k, shape)
    s_in = D_MODEL ** -0.5
    return {
        "x": nrm(ks[0], (BATCH, SEQ, D_MODEL)),
        "mem": nrm(ks[1], (BATCH, MEM_LEN, D_MODEL)),
        "positions": jnp.broadcast_to(jnp.arange(SEQ, dtype=jnp.int32), (BATCH, SEQ)),
        "a_norm": gain(ks[2], (N_A_LAYERS, D_MODEL)),
        "a_w_in": nrm(ks[3], (N_A_LAYERS, D_MODEL, A_IN)) * s_in,
        "a_q_norm": gain(ks[4], (N_A_LAYERS, HEAD_DIM)),
        "a_k_norm": gain(ks[5], (N_A_LAYERS, HEAD_DIM)),
        "a_lambda_q1": 0.1 * nrm(ks[6], (N_A_LAYERS, HEAD_DIM)),
        "a_lambda_k1": 0.1 * nrm(ks[7], (N_A_LAYERS, HEAD_DIM)),
        "a_lambda_q2": 0.1 * nrm(ks[8], (N_A_LAYERS, HEAD_DIM)),
        "a_lambda_k2": 0.1 * nrm(ks[9], (N_A_LAYERS, HEAD_DIM)),
        "a_subln": gain(ks[10], (N_A_LAYERS, 2 * HEAD_DIM)),
        "a_w_out": nrm(ks[11], (N_A_LAYERS, MIX_W, D_MODEL)) * (MIX_W ** -0.5),
        "kv_norm": gain(ks[12], (D_MODEL,)),
        "w_kv_shared": nrm(ks[13], (D_MODEL, 2 * SB_W)) * s_in,
        "b_norm": gain(ks[14], (N_B_LAYERS, D_MODEL)),
        "b_w_in": nrm(ks[15], (N_B_LAYERS, D_MODEL, B_IN)) * s_in,
        "b_w_out": nrm(ks[16], (N_B_LAYERS, MIX_W, D_MODEL)) * (MIX_W ** -0.5),
        "mem_norm": gain(ks[17], (DEPTH, D_MODEL)),
        "mem_w_kv": nrm(ks[18], (DEPTH, D_MODEL, 2 * MEM_W)) * s_in,
        "mem_q_norm": gain(ks[19], (DEPTH, MEM_HEAD_DIM)),
        "mem_k_norm": gain(ks[20], (DEPTH, MEM_HEAD_DIM)),
    }


def reference(x, mem, positions, a_norm, a_w_in, a_q_norm, a_k_norm, a_lambda_q1, a_lambda_k1,
              a_lambda_q2, a_lambda_k2, a_subln, a_w_out, kv_norm, w_kv_shared, b_norm, b_w_in,
              b_w_out, mem_norm, mem_w_kv, mem_q_norm, mem_k_norm):
    B, S, D = x.shape
    cos, sin = rope_cos_sin(positions)
    k_sb = v_sb = None
    for i in range(DEPTH):
        if i < N_A_LAYERS:
            l = i
            h = rms_norm(x, a_norm[l])
            proj = h @ a_w_in[l]
            q, k, v, qm, gate = jnp.split(
                proj, [DIFF_W, 2 * DIFF_W, 3 * DIFF_W, 3 * DIFF_W + MEM_W], axis=-1)
            q = apply_partial_rope(rms_norm(q.reshape(B, S, 2, DIFF_HEADS, HEAD_DIM), a_q_norm[l]), cos, sin)
            k = apply_partial_rope(rms_norm(k.reshape(B, S, 2, DIFF_HEADS, HEAD_DIM), a_k_norm[l]), cos, sin)
            q = q.transpose(0, 2, 3, 1, 4)
            k = k.transpose(0, 2, 3, 1, 4)
            v = v.reshape(B, S, DIFF_HEADS, 2 * HEAD_DIM).transpose(0, 2, 1, 3)
            lambda_init = 0.8 - 0.6 * math.exp(-0.3 * i)
            lam = (jnp.exp(jnp.sum(a_lambda_q1[l] * a_lambda_k1[l]).astype(jnp.float32))
                   - jnp.exp(jnp.sum(a_lambda_q2[l] * a_lambda_k2[l]).astype(jnp.float32))
                   + lambda_init)
            o = diff_attention(q, k, v, lam).transpose(0, 2, 1, 3)
            o = (rms_norm(o, a_subln[l]) * (1.0 - lambda_init)).reshape(B, S, DIFF_W)
            om = memory_attention(qm, mem, i, mem_norm, mem_w_kv, mem_q_norm, mem_k_norm)
            y = jnp.concatenate([o, om], axis=-1) * jax.nn.silu(gate)
            x = x + y @ a_w_out[l]
        else:
            j = i - N_A_LAYERS
            if i == N_A_LAYERS:
                kv = rms_norm(x, kv_norm) @ w_kv_shared
                k_s, v_s = jnp.split(kv, 2, axis=-1)
                k_sb = k_s.reshape(B, S, SB_HEADS, HEAD_DIM).transpose(0, 2, 1, 3)
                v_sb = v_s.reshape(B, S, SB_HEADS, HEAD_DIM).transpose(0, 2, 1, 3)
            h = rms_norm(x, b_norm[j])
            proj = h @ b_w_in[j]
            q, qm, gate = jnp.split(proj, [SB_W, SB_W + MEM_W], axis=-1)
            q = q.reshape(B, S, SB_HEADS, HEAD_DIM).transpose(0, 2, 1, 3)
            o = stick_breaking_attention(q, k_sb, v_sb).transpose(0, 2, 1, 3).reshape(B, S, SB_W)
            om = memory_attention(qm, mem, i, mem_norm, mem_w_kv, mem_q_norm, mem_k_norm)
            y = jnp.concatenate([o, om], axis=-1) * jax.nn.silu(gate)
            x = x + y @ b_w_out[j]
    return x
```

```python
import functools
import math

import jax
import jax.numpy as jnp
from jax import lax
from jax.experimental import pallas as pl
from jax.experimental.pallas import tpu as pltpu

HEAD_DIM = 64
DIFF_HEADS = 6
SB_HEADS = 12
MEM_HEADS = 4
ROPE_DIM = 16
ROPE_THETA = 500000.0
EPS = 1e-6

LANES = 128
MXU_DIM = 256
VMEM_LIMIT = 48 * 1024 * 1024

TILE = MXU_DIM
SCALE = HEAD_DIM ** -0.5


def _bf16(x):
    return x.astype(jnp.bfloat16)


def _dot(a, b):
    return jnp.dot(a, b, preferred_element_type=jnp.float32)


def _dot_nt(a, b):
    return lax.dot_general(a, b, (((1,), (1,)), ((), ())), preferred_element_type=jnp.float32)


def _rms_scale(x):
    return x * lax.rsqrt(jnp.mean(x * x, axis=-1, keepdims=True) + EPS)


def _group_norm(p, gmat, gain):
    sq = p * p
    hi = _bf16(sq)
    lo = _bf16(sq - hi.astype(jnp.float32))
    ss = _dot(hi, gmat) + _dot(lo, gmat)
    return p * lax.rsqrt(ss * (1.0 / HEAD_DIM) + EPS) * gain


def _silu(g):
    return g / (1.0 + jnp.exp(-g))


def _mem_attention(qm, km, vmh_ref):
    head_of_lane = lax.broadcasted_iota(jnp.int32, qm.shape, 1) // HEAD_DIM
    om = None
    for h in range(MEM_HEADS):
        qh = jnp.where(head_of_lane == h, qm, jnp.zeros_like(qm))
        sc = _dot_nt(qh, km)
        m = jnp.max(sc, axis=1, keepdims=True)
        p = jnp.exp(sc - m)
        p = p * (1.0 / jnp.sum(p, axis=1, keepdims=True))
        o = _dot(_bf16(p), vmh_ref[h])
        om = o if om is None else om + o
    return om


def _mem_prep_kernel(mem_ref, nrm_ref, w_ref, gk_ref, gmat_ref, lq1_ref, lk1_ref, lq2_ref,
                     lk2_ref, km_ref, vmh_ref, lam_ref):
    h = _bf16(_rms_scale(mem_ref[0]) * nrm_ref[0])
    kv = _dot(h, w_ref[0])
    mw = MEM_HEADS * HEAD_DIM
    km_ref[0, 0] = _bf16(_group_norm(kv[:, :mw], gmat_ref[...], gk_ref[0]))
    vm = _bf16(kv[:, mw:])
    head_of_lane = lax.broadcasted_iota(jnp.int32, vm.shape, 1) // HEAD_DIM
    for hh in range(MEM_HEADS):
        vmh_ref[0, 0, hh] = jnp.where(head_of_lane == hh, vm, jnp.zeros_like(vm))
    lambda_init = 0.8 - 0.6 * math.exp(-0.3 * 0)
    s1 = jnp.sum(lq1_ref[...] * lk1_ref[...], axis=1, keepdims=True)
    s2 = jnp.sum(lq2_ref[...] * lk2_ref[...], axis=1, keepdims=True)
    lam = jnp.exp(s1) - jnp.exp(s2) + lambda_init
    lam_ref[...] = jnp.broadcast_to(lam, lam_ref.shape)


def _mem_prep(mem, mem_norm, mem_w_kv, mem_k_norm, gmat, lq1, lk1, lq2, lk2):
    B, M, D = mem.shape
    L = mem_norm.shape[0]
    mw = MEM_HEADS * HEAD_DIM
    gk = jnp.tile(mem_k_norm, (1, MEM_HEADS)).reshape(L, 1, mw)
    vec = lambda: pl.BlockSpec((1, HEAD_DIM), lambda l, b: (0, 0))
    return pl.pallas_call(
        _mem_prep_kernel,
        grid=(L, B),
        in_specs=[
            pl.BlockSpec((1, M, D), lambda l, b: (b, 0, 0)),
            pl.BlockSpec((1, 1, D), lambda l, b: (l, 0, 0)),
            pl.BlockSpec((1, D, 2 * mw), lambda l, b: (l, 0, 0)),
            pl.BlockSpec((1, 1, mw), lambda l, b: (l, 0, 0)),
            pl.BlockSpec((mw, mw), lambda l, b: (0, 0)),
            vec(), vec(), vec(), vec(),
        ],
        out_specs=[
            pl.BlockSpec((1, 1, M, mw), lambda l, b: (l, b, 0, 0)),
            pl.BlockSpec((1, 1, MEM_HEADS, M, mw), lambda l, b: (l, b, 0, 0, 0)),
            pl.BlockSpec((8, TILE), lambda l, b: (0, 0)),
        ],
        out_shape=[
            jax.ShapeDtypeStruct((L, B, M, mw), jnp.bfloat16),
            jax.ShapeDtypeStruct((L, B, MEM_HEADS, M, mw), jnp.bfloat16),
            jax.ShapeDtypeStruct((8, TILE), jnp.float32),
        ],
        compiler_params=pltpu.CompilerParams(
            dimension_semantics=("arbitrary", "arbitrary"), vmem_limit_bytes=VMEM_LIMIT),
        name="mem_prep",
    )(mem, mem_norm.reshape(L, 1, D), _bf16(mem_w_kv), gk, gmat, lq1, lk1, lq2, lk2)


def _in_proj_a_kernel(x_ref, pos_ref, nrm_ref, wq_ref, wk_ref, wvt_ref, wqm_ref, wg_ref,
                      gq_ref, gk_ref, gqm_ref, inv_ref, slo_ref, shi_ref, gmat_ref,
                      q_out, k_out, vt_out, qm_out, sg_out):
    h = _bf16(_rms_scale(x_ref[0]) * nrm_ref[...])
    gmat = gmat_ref[...]

    ang = pos_ref[0] * inv_ref[...]
    cos = jnp.cos(ang)
    sin = jnp.sin(ang)
    sin_lo = sin * slo_ref[...]
    sin_hi = sin * shi_ref[...]

    def norm_rope(w_ref, g_ref, out_ref):
        p = _dot(h, w_ref[...])
        for c in range(p.shape[1] // MXU_DIM):
            cs = slice(c * MXU_DIM, (c + 1) * MXU_DIM)
            y = _group_norm(p[:, cs], gmat, g_ref[:, cs])
            for half in range(MXU_DIM // LANES):
                yb = y[:, half * LANES:(half + 1) * LANES]
                y_next = pltpu.roll(yb, LANES - ROPE_DIM // 2, axis=1)
                y_prev = pltpu.roll(yb, ROPE_DIM // 2, axis=1)
                lo = c * MXU_DIM + half * LANES
                out_ref[0, :, lo:lo + LANES] = _bf16(yb * cos + y_next * sin_lo + y_prev * sin_hi)

    norm_rope(wq_ref, gq_ref, q_out)
    norm_rope(wk_ref, gk_ref, k_out)
    vt_out[0, 0] = _bf16(_dot_nt(wvt_ref[...], h))
    qm_out[0] = _bf16(_group_norm(_dot(h, wqm_ref[...]), gmat, gqm_ref[...]))
    sg_out[0] = _bf16(_silu(_dot(h, wg_ref[...])))


def _in_proj_a(x, pos, nrm, wq, wk, wvt, wqm, wg, gq, gk, gqm, inv, slo, shi, gmat):
    B, S, D = x.shape
    dw = wq.shape[1]
    mw = wqm.shape[1]
    gw = wg.shape[1]
    full = lambda a: pl.BlockSpec(a.shape, lambda b, i: (0,) * a.ndim)
    row = lambda w: pl.BlockSpec((1, TILE, w), lambda b, i: (b, i, 0))
    consts = (nrm, wq, wk, wvt, wqm, wg, gq, gk, gqm, inv, slo, shi, gmat)
    return pl.pallas_call(
        _in_proj_a_kernel,
        grid=(B, S // TILE),
        in_specs=[row(D), row(1)] + [full(a) for a in consts],
        out_specs=[row(dw), row(dw),
                   pl.BlockSpec((1, 1, dw, TILE), lambda b, i: (b, i, 0, 0)),
                   row(mw), row(gw)],
        out_shape=[
            jax.ShapeDtypeStruct((B, S, dw), jnp.bfloat16),
            jax.ShapeDtypeStruct((B, S, dw), jnp.bfloat16),
            jax.ShapeDtypeStruct((B, S // TILE, dw, TILE), jnp.bfloat16),
            jax.ShapeDtypeStruct((B, S, mw), jnp.bfloat16),
            jax.ShapeDtypeStruct((B, S, gw), jnp.bfloat16),
        ],
        compiler_params=pltpu.CompilerParams(
            dimension_semantics=("arbitrary", "arbitrary"), vmem_limit_bytes=VMEM_LIMIT),
        name="in_proj_a",
    )(x, pos, *consts)


def _diff_attn_kernel(lam_ref, q_ref, k_ref, vt_ref, sg_ref, g_ref, o_ref, acc_ref):
    qi = pl.program_id(2)
    q = q_ref[0]
    lane = lax.broadcasted_iota(jnp.int32, q.shape, 1)
    zero = jnp.zeros_like(q)
    qmaps = (jnp.where(lane < HEAD_DIM, q, zero), jnp.where(lane >= HEAD_DIM, q, zero))

    def scores(kb, c):
        kblk = k_ref[0, pl.ds(pl.multiple_of(kb * TILE, TILE), TILE), :]
        return _dot_nt(kblk, qmaps[c])

    row = lax.broadcasted_iota(jnp.int32, (TILE, TILE), 0)
    col = lax.broadcasted_iota(jnp.int32, (TILE, TILE), 1)
    vt = vt_ref[0, qi]
    stats = []
    for c in range(2):
        s = jnp.where(row <= col, scores(qi, c), -jnp.inf)
        m = jnp.max(s, axis=0, keepdims=True)
        p = jnp.exp(s - m)
        acc_ref[c] = _dot(vt, _bf16(p))
        stats += [m, jnp.sum(p, axis=0, keepdims=True)]

    def body(kb, stats):
        vt = vt_ref[0, kb]
        new = []
        for c in range(2):
            m_prev, l_prev = stats[2 * c], stats[2 * c + 1]
            s = scores(kb, c)
            m_new = jnp.maximum(m_prev, jnp.max(s, axis=0, keepdims=True))
            p = jnp.exp(s - m_new)
            alpha = jnp.exp(m_prev - m_new)
            acc_ref[c] = alpha * acc_ref[c] + _dot(vt, _bf16(p))
            new += [m_new, alpha * l_prev + jnp.sum(p, axis=0, keepdims=True)]
        return tuple(new)

    m1, l1, m2, l2 = lax.fori_loop(0, qi, body, tuple(stats))

    lam = lam_ref[0:1, :]
    o = acc_ref[0] * (1.0 / l1) - acc_ref[1] * (lam * (1.0 / l2))
    o = o * lax.rsqrt(jnp.mean(o * o, axis=0, keepdims=True) + EPS) * g_ref[...]
    o_ref[0] = _bf16(o.T * sg_ref[0].astype(jnp.float32))


def _diff_attn(lam, q, k, vt, sg, gcol):
    B, S, dw = q.shape
    nt = S // TILE
    hw = 2 * HEAD_DIM
    return pl.pallas_call(
        _diff_attn_kernel,
        grid=(B, dw // hw, nt),
        in_specs=[
            pl.BlockSpec((8, TILE), lambda b, h, i: (0, 0)),
            pl.BlockSpec((1, TILE, hw), lambda b, h, i: (b, i, h)),
            pl.BlockSpec((1, S, hw), lambda b, h, i: (b, 0, h)),
            pl.BlockSpec((1, nt, hw, TILE), lambda b, h, i: (b, 0, h, 0)),
            pl.BlockSpec((1, TILE, hw), lambda b, h, i: (b, i, h)),
            pl.BlockSpec((hw, 1), lambda b, h, i: (0, 0)),
        ],
        out_specs=pl.BlockSpec((1, TILE, hw), lambda b, h, i: (b, i, h)),
        out_shape=jax.ShapeDtypeStruct((B, S, dw), jnp.bfloat16),
        scratch_shapes=[pltpu.VMEM((2, hw, TILE), jnp.float32)],
        compiler_params=pltpu.CompilerParams(
            dimension_semantics=("arbitrary", "arbitrary", "arbitrary"),
            vmem_limit_bytes=VMEM_LIMIT),
        name="diff_attn",
    )(lam, q, k, vt, sg, gcol)


def _mid_kernel(yo_ref, qm_ref, sgm_ref, x_ref, km_ref, vmh_ref, wo_ref, gkv_ref, wk_ref,
                wvt_ref, gb_ref, wq_ref, wqm_ref, wg_ref, gqm_ref, gmat_ref,
                x1_out, k_out, vt_out, q_out, qm_out, sg_out):
    dw = yo_ref.shape[2]
    om = _mem_attention(qm_ref[0], km_ref[0, 0], vmh_ref.at[0, 0])
    ym = _bf16(om * sgm_ref[0].astype(jnp.float32))
    x1 = x_ref[0] + _dot(yo_ref[0], wo_ref[:dw, :]) + _dot(ym, wo_ref[dw:, :])
    x1_out[0] = x1
    xn = _rms_scale(x1)
    hk = _bf16(xn * gkv_ref[...])
    k_out[0] = _bf16(_dot(hk, wk_ref[...]))
    vt_out[0, 0] = _bf16(_dot_nt(wvt_ref[...], hk))
    hb = _bf16(xn * gb_ref[...])
    q_out[0] = _bf16(_dot(hb, wq_ref[...]))
    qm_out[0] = _bf16(_group_norm(_dot(hb, wqm_ref[...]), gmat_ref[...], gqm_ref[...]))
    sg_out[0] = _bf16(_silu(_dot(hb, wg_ref[...])))


def _mid(yo, qm, sg, x, km, vmh, wo, gkv, wk, wvt, gb, wq, wqm, wg, gqm, gmat):
    B, S, D = x.shape
    dw = yo.shape[2]
    mw = qm.shape[2]
    gw = wg.shape[1]
    M = km.shape[2]
    full = lambda a: pl.BlockSpec(a.shape, lambda b, i: (0,) * a.ndim)
    row = lambda w: pl.BlockSpec((1, TILE, w), lambda b, i: (b, i, 0))
    consts = (wo, gkv, wk, wvt, gb, wq, wqm, wg, gqm, gmat)
    return pl.pallas_call(
        _mid_kernel,
        grid=(B, S // TILE),
        in_specs=[row(dw), row(mw),
                  pl.BlockSpec((1, TILE, mw), lambda b, i: (b, i, dw // mw)),
                  row(D),
                  pl.BlockSpec((1, 1, M, mw), lambda b, i: (0, b, 0, 0)),
                  pl.BlockSpec((1, 1, MEM_HEADS, M, mw), lambda b, i: (0, b, 0, 0, 0)),
                  ] + [full(a) for a in consts],
        out_specs=[row(D), row(dw),
                   pl.BlockSpec((1, 1, dw, TILE), lambda b, i: (b, i, 0, 0)),
                   row(dw), row(mw), row(gw)],
        out_shape=[
            jax.ShapeDtypeStruct((B, S, D), jnp.float32),
            jax.ShapeDtypeStruct((B, S, dw), jnp.bfloat16),
            jax.ShapeDtypeStruct((B, S // TILE, dw, TILE), jnp.bfloat16),
            jax.ShapeDtypeStruct((B, S, dw), jnp.bfloat16),
            jax.ShapeDtypeStruct((B, S, mw), jnp.bfloat16),
            jax.ShapeDtypeStruct((B, S, gw), jnp.bfloat16),
        ],
        compiler_params=pltpu.CompilerParams(
            dimension_semantics=("arbitrary", "arbitrary"), vmem_limit_bytes=VMEM_LIMIT),
        name="mid",
    )(yo, qm, sg, x, km, vmh, *consts)


def _sb_attn_kernel(q_ref, k_ref, vt_ref, sg_ref, tri_ref, o_ref, acc_ref):
    qi = pl.program_id(2)
    q = q_ref[0]
    lane = lax.broadcasted_iota(jnp.int32, q.shape, 1)
    zero = jnp.zeros_like(q)
    qheads = (jnp.where(lane < HEAD_DIM, q, zero), jnp.where(lane >= HEAD_DIM, q, zero))
    tri = tri_ref[...]

    def tile(kb, e, valid):
        kblk = k_ref[0, pl.ds(pl.multiple_of(kb * TILE, TILE), TILE), :]
        z = _dot_nt(kblk, qheads[e])
        sp = jnp.maximum(z, 0.0) + jnp.log(1.0 + jnp.exp(-jnp.abs(z)))
        if valid is not None:
            sp = jnp.where(valid, sp, 0.0)
        hi = _bf16(sp)
        lo = _bf16(sp - hi.astype(jnp.float32))
        csum = _dot(tri, hi) + _dot(tri, lo)
        return z, csum

    row = lax.broadcasted_iota(jnp.int32, (TILE, TILE), 0)
    col = lax.broadcasted_iota(jnp.int32, (TILE, TILE), 1)
    valid = row < col
    vt = vt_ref[0, qi]
    carries = []
    for e in range(2):
        z, csum = tile(qi, e, valid)
        a = jnp.where(valid, jnp.exp(z - csum), 0.0)
        rows = slice(e * HEAD_DIM, (e + 1) * HEAD_DIM)
        acc_ref[rows, :] = _dot(vt[rows, :], _bf16(a))
        carries.append(csum[0:1, :])

    def body(i, carries):
        kb = qi - 1 - i
        vt = vt_ref[0, kb]
        new = []
        for e in range(2):
            z, csum = tile(kb, e, None)
            a = jnp.exp(z - csum - carries[e])
            rows = slice(e * HEAD_DIM, (e + 1) * HEAD_DIM)
            acc_ref[rows, :] += _dot(vt[rows, :], _bf16(a))
            new.append(carries[e] + csum[0:1, :])
        return tuple(new)

    lax.fori_loop(0, qi, body, tuple(carries))
    o_ref[0] = _bf16(acc_ref[...].T * sg_ref[0].astype(jnp.float32))


def _sb_attn(q, k, vt, sg, tri):
    B, S, dw = q.shape
    nt = S // TILE
    hw = 2 * HEAD_DIM
    return pl.pallas_call(
        _sb_attn_kernel,
        grid=(B, dw // hw, nt),
        in_specs=[
            pl.BlockSpec((1, TILE, hw), lambda b, h, i: (b, i, h)),
            pl.BlockSpec((1, S, hw), lambda b, h, i: (b, 0, h)),
            pl.BlockSpec((1, nt, hw, TILE), lambda b, h, i: (b, 0, h, 0)),
            pl.BlockSpec((1, TILE, hw), lambda b, h, i: (b, i, h)),
            pl.BlockSpec((TILE, TILE), lambda b, h, i: (0, 0)),
        ],
        out_specs=pl.BlockSpec((1, TILE, hw), lambda b, h, i: (b, i, h)),
        out_shape=jax.ShapeDtypeStruct((B, S, dw), jnp.bfloat16),
        scratch_shapes=[pltpu.VMEM((hw, TILE), jnp.float32)],
        compiler_params=pltpu.CompilerParams(
            dimension_semantics=("arbitrary", "arbitrary", "arbitrary"),
            vmem_limit_bytes=VMEM_LIMIT),
        name="sb_attn",
    )(q, k, vt, sg, tri)


def _out_b_kernel(yo_ref, qm_ref, sgm_ref, x_ref, km_ref, vmh_ref, wo_ref, out_ref):
    dw = yo_ref.shape[2]
    om = _mem_attention(qm_ref[0], km_ref[0, 0], vmh_ref.at[0, 0])
    ym = _bf16(om * sgm_ref[0].astype(jnp.float32))
    out_ref[0] = x_ref[0] + _dot(yo_ref[0], wo_ref[:dw, :]) + _dot(ym, wo_ref[dw:, :])


def _out_b(yo, qm, sg, x, km, vmh, wo):
    B, S, D = x.shape
    dw = yo.shape[2]
    mw = qm.shape[2]
    M = km.shape[2]
    row = lambda w: pl.BlockSpec((1, TILE, w), lambda b, i: (b, i, 0))
    return pl.pallas_call(
        _out_b_kernel,
        grid=(B, S // TILE),
        in_specs=[row(dw), row(mw),
                  pl.BlockSpec((1, TILE, mw), lambda b, i: (b, i, dw // mw)),
                  row(D),
                  pl.BlockSpec((1, 1, M, mw), lambda b, i: (1, b, 0, 0)),
                  pl.BlockSpec((1, 1, MEM_HEADS, M, mw), lambda b, i: (1, b, 0, 0, 0)),
                  pl.BlockSpec(wo.shape, lambda b, i: (0, 0))],
        out_specs=row(D),
        out_shape=jax.ShapeDtypeStruct((B, S, D), jnp.float32),
        compiler_params=pltpu.CompilerParams(
            dimension_semantics=("arbitrary", "arbitrary"), vmem_limit_bytes=VMEM_LIMIT),
        name="out_b",
    )(yo, qm, sg, x, km, vmh, wo)


def kernel(x, mem, positions, a_norm, a_w_in, a_q_norm, a_k_norm, a_lambda_q1, a_lambda_k1,
           a_lambda_q2, a_lambda_k2, a_subln, a_w_out, kv_norm, w_kv_shared, b_norm, b_w_in,
           b_w_out, mem_norm, mem_w_kv, mem_q_norm, mem_k_norm):
    B, S, D = x.shape
    dw = DIFF_HEADS * 2 * HEAD_DIM
    mw = MEM_HEADS * HEAD_DIM
    f32 = jnp.float32

    grp = jnp.arange(MXU_DIM) // HEAD_DIM
    gmat = _bf16(grp[:, None] == grp[None, :])
    tri = _bf16(jnp.triu(jnp.ones((TILE, TILE), f32)))
    d = jnp.arange(LANES) % HEAD_DIM
    inv_freq = ROPE_THETA ** (-jnp.arange(0, ROPE_DIM, 2, dtype=f32) / ROPE_DIM)
    inv = jnp.where(d < ROPE_DIM, inv_freq[d % (ROPE_DIM // 2)], 0.0).reshape(1, LANES)
    slo = jnp.where(d < ROPE_DIM // 2, -1.0, 0.0).astype(f32).reshape(1, LANES)
    shi = jnp.where((d >= ROPE_DIM // 2) & (d < ROPE_DIM), 1.0, 0.0).astype(f32).reshape(1, LANES)

    wa = a_w_in[0]
    regroup = lambda w: w.reshape(D, 2, DIFF_HEADS, HEAD_DIM).transpose(0, 2, 1, 3).reshape(D, dw)
    wq = _bf16(regroup(wa[:, :dw]))
    wk = _bf16(regroup(wa[:, dw:2 * dw]))
    wvt = _bf16(wa[:, 2 * dw:3 * dw].T)
    wqm = _bf16(wa[:, 3 * dw:3 * dw + mw])
    wg = _bf16(wa[:, 3 * dw + mw:])
    gq = (jnp.tile(a_q_norm[0], 2 * DIFF_HEADS) * SCALE).reshape(1, dw)
    gk = jnp.tile(a_k_norm[0], 2 * DIFF_HEADS).reshape(1, dw)
    gqm0 = (jnp.tile(mem_q_norm[0], MEM_HEADS) * SCALE).reshape(1, mw)
    gqm1 = (jnp.tile(mem_q_norm[1], MEM_HEADS) * SCALE).reshape(1, mw)
    lambda_init = 0.8 - 0.6 * math.exp(-0.3 * 0)
    gsub = (a_subln[0] * (1.0 - lambda_init)).reshape(2 * HEAD_DIM, 1)

    km, vmh, lam = _mem_prep(mem, mem_norm, mem_w_kv, mem_k_norm, gmat,
                             a_lambda_q1, a_lambda_k1, a_lambda_q2, a_lambda_k2)

    pos = positions.astype(f32).reshape(B, S, 1)
    q0, k0, vt0, qm0, sg0 = _in_proj_a(x, pos, a_norm[0].reshape(1, D), wq, wk, wvt, wqm, wg,
                                       gq, gk, gqm0, inv, slo, shi, gmat)
    y0 = _diff_attn(lam, q0, k0, vt0, sg0, gsub)

    wb = b_w_in[0]
    x1, k1, vt1, q1, qm1, sg1 = _mid(
        y0, qm0, sg0, x, km, vmh, _bf16(a_w_out[0]), kv_norm.reshape(1, D),
        _bf16(w_kv_shared[:, :dw]), _bf16(w_kv_shared[:, dw:].T), b_norm[0].reshape(1, D),
        _bf16(wb[:, :dw] * SCALE), _bf16(wb[:, dw:dw + mw]), _bf16(wb[:, dw + mw:]), gqm1, gmat)
    y1 = _sb_attn(q1, k1, vt1, sg1, tri)
    return _out_b(y1, qm1, sg1, x1, km, vmh, _bf16(b_w_out[0]))
```

```python
import functools
import math

import jax
import jax.numpy as jnp
from jax import lax
from jax.experimental import pallas as pl
from jax.experimental.pallas import tpu as pltpu

HEAD_DIM = 64
DIFF_HEADS = 6
SB_HEADS = 12
MEM_HEADS = 4
ROPE_DIM = 16
ROPE_THETA = 500000.0
EPS = 1e-6

LANES = 128
MXU_DIM = 256
VMEM_LIMIT = 48 * 1024 * 1024

TILE = MXU_DIM
KEY_TILES_PER_STEP = 4
SB_CUTOFF = 105.0


def _key_chunk(seq):
    return min(KEY_TILES_PER_STEP * TILE, seq)
SCALE = HEAD_DIM ** -0.5


def _bf16(x):
    return x.astype(jnp.bfloat16)


def _dot(a, b):
    return jnp.dot(a, b, preferred_element_type=jnp.float32)


def _dot_nt(a, b):
    return lax.dot_general(a, b, (((1,), (1,)), ((), ())), preferred_element_type=jnp.float32)


def _rms_scale(x):
    return x * lax.rsqrt(jnp.mean(x * x, axis=-1, keepdims=True) + EPS)


def _group_norm(p, gmat, gain):
    sq = p * p
    hi = _bf16(sq)
    lo = _bf16(sq - hi.astype(jnp.float32))
    ss = _dot(hi, gmat) + _dot(lo, gmat)
    return p * lax.rsqrt(ss * (1.0 / HEAD_DIM) + EPS) * gain


def _silu(g):
    return g / (1.0 + jnp.exp(-g))


def _mem_attention(qm, km, vmh_ref):
    head_of_lane = lax.broadcasted_iota(jnp.int32, qm.shape, 1) // HEAD_DIM
    om = None
    for h in range(MEM_HEADS):
        qh = jnp.where(head_of_lane == h, qm, jnp.zeros_like(qm))
        sc = _dot_nt(qh, km)
        m = jnp.max(sc, axis=1, keepdims=True)
        p = jnp.exp(sc - m)
        p = p * (1.0 / jnp.sum(p, axis=1, keepdims=True))
        o = _dot(_bf16(p), vmh_ref[h])
        om = o if om is None else om + o
    return om


def _mem_prep_kernel(mem_ref, nrm_ref, w_ref, gk_ref, gmat_ref, lq1_ref, lk1_ref, lq2_ref,
                     lk2_ref, km_ref, vmh_ref, lam_ref):
    h = _bf16(_rms_scale(mem_ref[0]) * nrm_ref[0])
    kv = _dot(h, w_ref[0])
    mw = MEM_HEADS * HEAD_DIM
    km_ref[0, 0] = _bf16(_group_norm(kv[:, :mw], gmat_ref[...], gk_ref[0]))
    vm = _bf16(kv[:, mw:])
    head_of_lane = lax.broadcasted_iota(jnp.int32, vm.shape, 1) // HEAD_DIM
    for hh in range(MEM_HEADS):
        vmh_ref[0, 0, hh] = jnp.where(head_of_lane == hh, vm, jnp.zeros_like(vm))
    lambda_init = 0.8 - 0.6 * math.exp(-0.3 * 0)
    s1 = jnp.sum(lq1_ref[...] * lk1_ref[...], axis=1, keepdims=True)
    s2 = jnp.sum(lq2_ref[...] * lk2_ref[...], axis=1, keepdims=True)
    lam = jnp.exp(s1) - jnp.exp(s2) + lambda_init
    lam_ref[...] = jnp.broadcast_to(lam, lam_ref.shape)


def _mem_prep(mem, mem_norm, mem_w_kv, mem_k_norm, gmat, lq1, lk1, lq2, lk2):
    B, M, D = mem.shape
    L = mem_norm.shape[0]
    mw = MEM_HEADS * HEAD_DIM
    gk = jnp.tile(mem_k_norm, (1, MEM_HEADS)).reshape(L, 1, mw)
    vec = lambda: pl.BlockSpec((1, HEAD_DIM), lambda l, b: (0, 0))
    return pl.pallas_call(
        _mem_prep_kernel,
        grid=(L, B),
        in_specs=[
            pl.BlockSpec((1, M, D), lambda l, b: (b, 0, 0)),
            pl.BlockSpec((1, 1, D), lambda l, b: (l, 0, 0)),
            pl.BlockSpec((1, D, 2 * mw), lambda l, b: (l, 0, 0)),
            pl.BlockSpec((1, 1, mw), lambda l, b: (l, 0, 0)),
            pl.BlockSpec((mw, mw), lambda l, b: (0, 0)),
            vec(), vec(), vec(), vec(),
        ],
        out_specs=[
            pl.BlockSpec((1, 1, M, mw), lambda l, b: (l, b, 0, 0)),
            pl.BlockSpec((1, 1, MEM_HEADS, M, mw), lambda l, b: (l, b, 0, 0, 0)),
            pl.BlockSpec((8, TILE), lambda l, b: (0, 0)),
        ],
        out_shape=[
            jax.ShapeDtypeStruct((L, B, M, mw), jnp.bfloat16),
            jax.ShapeDtypeStruct((L, B, MEM_HEADS, M, mw), jnp.bfloat16),
            jax.ShapeDtypeStruct((8, TILE), jnp.float32),
        ],
        compiler_params=pltpu.CompilerParams(
            dimension_semantics=("arbitrary", "arbitrary"), vmem_limit_bytes=VMEM_LIMIT),
        name="mem_prep",
    )(mem, mem_norm.reshape(L, 1, D), _bf16(mem_w_kv), gk, gmat, lq1, lk1, lq2, lk2)


def _in_proj_a_kernel(x_ref, pos_ref, nrm_ref, wq_ref, wk_ref, wvt_ref, wqm_ref, wg_ref,
                      gq_ref, gk_ref, gqm_ref, inv_ref, slo_ref, shi_ref, gmat_ref,
                      q_out, k_out, vt_out, qm_out, sg_out):
    h = _bf16(_rms_scale(x_ref[0]) * nrm_ref[...])
    gmat = gmat_ref[...]

    ang = pos_ref[0] * inv_ref[...]
    cos = jnp.cos(ang)
    sin = jnp.sin(ang)
    sin_lo = sin * slo_ref[...]
    sin_hi = sin * shi_ref[...]

    def norm_rope(w_ref, g_ref, out_ref):
        p = _dot(h, w_ref[...])
        for c in range(p.shape[1] // MXU_DIM):
            cs = slice(c * MXU_DIM, (c + 1) * MXU_DIM)
            y = _group_norm(p[:, cs], gmat, g_ref[:, cs])
            for half in range(MXU_DIM // LANES):
                yb = y[:, half * LANES:(half + 1) * LANES]
                y_next = pltpu.roll(yb, LANES - ROPE_DIM // 2, axis=1)
                y_prev = pltpu.roll(yb, ROPE_DIM // 2, axis=1)
                lo = c * MXU_DIM + half * LANES
                out_ref[0, :, lo:lo + LANES] = _bf16(yb * cos + y_next * sin_lo + y_prev * sin_hi)

    norm_rope(wq_ref, gq_ref, q_out)
    norm_rope(wk_ref, gk_ref, k_out)
    vt_out[0, 0] = _bf16(_dot_nt(wvt_ref[...], h))
    qm_out[0] = _bf16(_group_norm(_dot(h, wqm_ref[...]), gmat, gqm_ref[...]))
    sg_out[0] = _bf16(_silu(_dot(h, wg_ref[...])))


def _in_proj_a(x, pos, nrm, wq, wk, wvt, wqm, wg, gq, gk, gqm, inv, slo, shi, gmat):
    B, S, D = x.shape
    tpc = _key_chunk(S) // TILE
    dw = wq.shape[1]
    mw = wqm.shape[1]
    gw = wg.shape[1]
    full = lambda a: pl.BlockSpec(a.shape, lambda b, i: (0,) * a.ndim)
    row = lambda w: pl.BlockSpec((1, TILE, w), lambda b, i: (b, i, 0))
    consts = (nrm, wq, wk, wvt, wqm, wg, gq, gk, gqm, inv, slo, shi, gmat)
    return pl.pallas_call(
        _in_proj_a_kernel,
        grid=(B, S // TILE),
        in_specs=[row(D), row(1)] + [full(a) for a in consts],
        out_specs=[row(dw), row(dw),
                   pl.BlockSpec((1, 1, dw, TILE), lambda b, i: (b, i // tpc, 0, i % tpc)),
                   row(mw), row(gw)],
        out_shape=[
            jax.ShapeDtypeStruct((B, S, dw), jnp.bfloat16),
            jax.ShapeDtypeStruct((B, S, dw), jnp.bfloat16),
            jax.ShapeDtypeStruct((B, S // (tpc * TILE), dw, tpc * TILE), jnp.bfloat16),
            jax.ShapeDtypeStruct((B, S, mw), jnp.bfloat16),
            jax.ShapeDtypeStruct((B, S, gw), jnp.bfloat16),
        ],
        compiler_params=pltpu.CompilerParams(
            dimension_semantics=("arbitrary", "arbitrary"), vmem_limit_bytes=VMEM_LIMIT),
        name="in_proj_a",
    )(x, pos, *consts)


def _diff_attn_kernel(lam_ref, q_ref, k_ref, vt_ref, sg_ref, g_ref, o_ref, acc_ref, *, kchunk):
    qi = pl.program_id(2)
    tiles_per_chunk = kchunk // TILE
    dc = qi // tiles_per_chunk
    q = q_ref[0]
    lane = lax.broadcasted_iota(jnp.int32, q.shape, 1)
    zero = jnp.zeros_like(q)
    qmaps = (jnp.where(lane < HEAD_DIM, q, zero), jnp.where(lane >= HEAD_DIM, q, zero))

    def scores(kc, c):
        kblk = k_ref[0, pl.ds(pl.multiple_of(kc * kchunk, kchunk), kchunk), :]
        return _dot_nt(kblk, qmaps[c])

    row = lax.broadcasted_iota(jnp.int32, (kchunk, TILE), 0)
    col = lax.broadcasted_iota(jnp.int32, (kchunk, TILE), 1)
    valid = row - col <= (qi - dc * tiles_per_chunk) * TILE
    vt = vt_ref[0, dc]
    stats = []
    for c in range(2):
        s = jnp.where(valid, scores(dc, c), -jnp.inf)
        m = jnp.max(s, axis=0, keepdims=True)
        p = jnp.exp(s - m)
        acc_ref[c] = _dot(vt, _bf16(p))
        stats += [m, jnp.sum(p, axis=0, keepdims=True)]

    def body(kc, stats):
        vt = vt_ref[0, kc]
        new = []
        for c in range(2):
            m_prev, l_prev = stats[2 * c], stats[2 * c + 1]
            s = scores(kc, c)
            m_new = jnp.maximum(m_prev, jnp.max(s, axis=0, keepdims=True))
            p = jnp.exp(s - m_new)
            alpha = jnp.exp(m_prev - m_new)
            acc_ref[c] = alpha * acc_ref[c] + _dot(vt, _bf16(p))
            new += [m_new, alpha * l_prev + jnp.sum(p, axis=0, keepdims=True)]
        return tuple(new)

    m1, l1, m2, l2 = lax.fori_loop(0, dc, body, tuple(stats))

    lam = lam_ref[0:1, :]
    o = acc_ref[0] * (1.0 / l1) - acc_ref[1] * (lam * (1.0 / l2))
    o = o * lax.rsqrt(jnp.mean(o * o, axis=0, keepdims=True) + EPS) * g_ref[...]
    o_ref[0] = _bf16(o.T * sg_ref[0].astype(jnp.float32))


def _diff_attn(lam, q, k, vt, sg, gcol):
    B, S, dw = q.shape
    nt = S // TILE
    hw = 2 * HEAD_DIM
    nc, kchunk = vt.shape[1], vt.shape[3]
    return pl.pallas_call(
        functools.partial(_diff_attn_kernel, kchunk=kchunk),
        grid=(B, dw // hw, nt),
        in_specs=[
            pl.BlockSpec((8, TILE), lambda b, h, i: (0, 0)),
            pl.BlockSpec((1, TILE, hw), lambda b, h, i: (b, i, h)),
            pl.BlockSpec((1, S, hw), lambda b, h, i: (b, 0, h)),
            pl.BlockSpec((1, nc, hw, kchunk), lambda b, h, i: (b, 0, h, 0)),
            pl.BlockSpec((1, TILE, hw), lambda b, h, i: (b, i, h)),
            pl.BlockSpec((hw, 1), lambda b, h, i: (0, 0)),
        ],
        out_specs=pl.BlockSpec((1, TILE, hw), lambda b, h, i: (b, i, h)),
        out_shape=jax.ShapeDtypeStruct((B, S, dw), jnp.bfloat16),
        scratch_shapes=[pltpu.VMEM((2, hw, TILE), jnp.float32)],
        compiler_params=pltpu.CompilerParams(
            dimension_semantics=("arbitrary", "arbitrary", "arbitrary"),
            vmem_limit_bytes=VMEM_LIMIT),
        name="diff_attn",
    )(lam, q, k, vt, sg, gcol)


def _mid_kernel(yo_ref, qm_ref, sgm_ref, x_ref, km_ref, vmh_ref, wo_ref, gkv_ref, wk_ref,
                wvt_ref, gb_ref, wq_ref, wqm_ref, wg_ref, gqm_ref, gmat_ref,
                x1_out, k_out, vt_out, q_out, qm_out, sg_out):
    dw = yo_ref.shape[2]
    om = _mem_attention(qm_ref[0], km_ref[0, 0], vmh_ref.at[0, 0])
    ym = _bf16(om * sgm_ref[0].astype(jnp.float32))
    x1 = x_ref[0] + _dot(yo_ref[0], wo_ref[:dw, :]) + _dot(ym, wo_ref[dw:, :])
    x1_out[0] = x1
    xn = _rms_scale(x1)
    hk = _bf16(xn * gkv_ref[...])
    k_out[0] = _bf16(_dot(hk, wk_ref[...]))
    vt_out[0, 0] = _bf16(_dot_nt(wvt_ref[...], hk))
    hb = _bf16(xn * gb_ref[...])
    q_out[0] = _bf16(_dot(hb, wq_ref[...]))
    qm_out[0] = _bf16(_group_norm(_dot(hb, wqm_ref[...]), gmat_ref[...], gqm_ref[...]))
    sg_out[0] = _bf16(_silu(_dot(hb, wg_ref[...])))


def _mid(yo, qm, sg, x, km, vmh, wo, gkv, wk, wvt, gb, wq, wqm, wg, gqm, gmat):
    B, S, D = x.shape
    tpc = _key_chunk(S) // TILE
    dw = yo.shape[2]
    mw = qm.shape[2]
    gw = wg.shape[1]
    M = km.shape[2]
    full = lambda a: pl.BlockSpec(a.shape, lambda b, i: (0,) * a.ndim)
    row = lambda w: pl.BlockSpec((1, TILE, w), lambda b, i: (b, i, 0))
    consts = (wo, gkv, wk, wvt, gb, wq, wqm, wg, gqm, gmat)
    return pl.pallas_call(
        _mid_kernel,
        grid=(B, S // TILE),
        in_specs=[row(dw), row(mw),
                  pl.BlockSpec((1, TILE, mw), lambda b, i: (b, i, dw // mw)),
                  row(D),
                  pl.BlockSpec((1, 1, M, mw), lambda b, i: (0, b, 0, 0)),
                  pl.BlockSpec((1, 1, MEM_HEADS, M, mw), lambda b, i: (0, b, 0, 0, 0)),
                  ] + [full(a) for a in consts],
        out_specs=[row(D), row(dw),
                   pl.BlockSpec((1, 1, dw, TILE), lambda b, i: (b, i // tpc, 0, i % tpc)),
                   row(dw), row(mw), row(gw)],
        out_shape=[
            jax.ShapeDtypeStruct((B, S, D), jnp.float32),
            jax.ShapeDtypeStruct((B, S, dw), jnp.bfloat16),
            jax.ShapeDtypeStruct((B, S // (tpc * TILE), dw, tpc * TILE), jnp.bfloat16),
            jax.ShapeDtypeStruct((B, S, dw), jnp.bfloat16),
            jax.ShapeDtypeStruct((B, S, mw), jnp.bfloat16),
            jax.ShapeDtypeStruct((B, S, gw), jnp.bfloat16),
        ],
        compiler_params=pltpu.CompilerParams(
            dimension_semantics=("arbitrary", "arbitrary"), vmem_limit_bytes=VMEM_LIMIT),
        name="mid",
    )(yo, qm, sg, x, km, vmh, *consts)


def _sb_attn_kernel(q_ref, k_ref, vt_ref, sg_ref, tri_ref, o_ref, acc_ref, *, kchunk):
    qi = pl.program_id(2)
    tiles_per_chunk = kchunk // TILE
    dc = qi // tiles_per_chunk
    q = q_ref[0]
    lane = lax.broadcasted_iota(jnp.int32, q.shape, 1)
    zero = jnp.zeros_like(q)
    qheads = (jnp.where(lane < HEAD_DIM, q, zero), jnp.where(lane >= HEAD_DIM, q, zero))
    tri = tri_ref[...]

    def chunk(kc, carries, valid, first):
        kblk = k_ref[0, pl.ds(pl.multiple_of(kc * kchunk, kchunk), kchunk), :]
        vt = vt_ref[0, kc]
        new = []
        for e in range(2):
            z = _dot_nt(kblk, qheads[e])
            sp = jnp.maximum(z, 0.0) + jnp.log(1.0 + jnp.exp(-jnp.abs(z)))
            if valid is not None:
                sp = jnp.where(valid, sp, 0.0)
            hi = _bf16(sp)
            lo = _bf16(sp - hi.astype(jnp.float32))
            rows = slice(e * HEAD_DIM, (e + 1) * HEAD_DIM)
            right = carries[e]
            contrib = None
            for j in reversed(range(tiles_per_chunk)):
                ks = slice(j * TILE, (j + 1) * TILE)
                csum = _dot(tri, hi[ks]) + _dot(tri, lo[ks])
                a = jnp.exp(z[ks] - csum - right)
                if valid is not None:
                    a = jnp.where(valid[ks], a, 0.0)
                o = _dot(vt[rows, ks], _bf16(a))
                contrib = o if contrib is None else contrib + o
                right = right + csum[0:1, :]
            if first:
                acc_ref[rows, :] = contrib
            else:
                acc_ref[rows, :] += contrib
            new.append(right)
        return tuple(new)

    def unfinished(carries):
        smallest = jnp.min(jnp.minimum(carries[0], carries[1]), axis=1, keepdims=True)
        return (smallest[0, 0] <= SB_CUTOFF).astype(jnp.int32)

    row = lax.broadcasted_iota(jnp.int32, (kchunk, TILE), 0)
    col = lax.broadcasted_iota(jnp.int32, (kchunk, TILE), 1)
    valid = row - col < (qi - dc * tiles_per_chunk) * TILE
    zeros = jnp.zeros((1, TILE), jnp.float32)
    carries = chunk(dc, (zeros, zeros), valid, True)

    def cond(state):
        return jnp.logical_and(state[0] >= 0, state[1] > 0)

    def body(state):
        new = chunk(state[0], (state[2], state[3]), None, False)
        return (state[0] - 1, unfinished(new)) + new

    lax.while_loop(cond, body, (dc - 1, unfinished(carries)) + carries)
    o_ref[0] = _bf16(acc_ref[...].T * sg_ref[0].astype(jnp.float32))


def _sb_attn(q, k, vt, sg, tri):
    B, S, dw = q.shape
    nt = S // TILE
    hw = 2 * HEAD_DIM
    nc, kchunk = vt.shape[1], vt.shape[3]
    return pl.pallas_call(
        functools.partial(_sb_attn_kernel, kchunk=kchunk),
        grid=(B, dw // hw, nt),
        in_specs=[
            pl.BlockSpec((1, TILE, hw), lambda b, h, i: (b, i, h)),
            pl.BlockSpec((1, S, hw), lambda b, h, i: (b, 0, h)),
            pl.BlockSpec((1, nc, hw, kchunk), lambda b, h, i: (b, 0, h, 0)),
            pl.BlockSpec((1, TILE, hw), lambda b, h, i: (b, i, h)),
            pl.BlockSpec((TILE, TILE), lambda b, h, i: (0, 0)),
        ],
        out_specs=pl.BlockSpec((1, TILE, hw), lambda b, h, i: (b, i, h)),
        out_shape=jax.ShapeDtypeStruct((B, S, dw), jnp.bfloat16),
        scratch_shapes=[pltpu.VMEM((hw, TILE), jnp.float32)],
        compiler_params=pltpu.CompilerParams(
            dimension_semantics=("arbitrary", "arbitrary", "arbitrary"),
            vmem_limit_bytes=VMEM_LIMIT),
        name="sb_attn",
    )(q, k, vt, sg, tri)


def _out_b_kernel(yo_ref, qm_ref, sgm_ref, x_ref, km_ref, vmh_ref, wo_ref, out_ref):
    dw = yo_ref.shape[2]
    om = _mem_attention(qm_ref[0], km_ref[0, 0], vmh_ref.at[0, 0])
    ym = _bf16(om * sgm_ref[0].astype(jnp.float32))
    out_ref[0] = x_ref[0] + _dot(yo_ref[0], wo_ref[:dw, :]) + _dot(ym, wo_ref[dw:, :])


def _out_b(yo, qm, sg, x, km, vmh, wo):
    B, S, D = x.shape
    dw = yo.shape[2]
    mw = qm.shape[2]
    M = km.shape[2]
    row = lambda w: pl.BlockSpec((1, TILE, w), lambda b, i: (b, i, 0))
    return pl.pallas_call(
        _out_b_kernel,
        grid=(B, S // TILE),
        in_specs=[row(dw), row(mw),
                  pl.BlockSpec((1, TILE, mw), lambda b, i: (b, i, dw // mw)),
                  row(D),
                  pl.BlockSpec((1, 1, M, mw), lambda b, i: (1, b, 0, 0)),
                  pl.BlockSpec((1, 1, MEM_HEADS, M, mw), lambda b, i: (1, b, 0, 0, 0)),
                  pl.BlockSpec(wo.shape, lambda b, i: (0, 0))],
        out_specs=row(D),
        out_shape=jax.ShapeDtypeStruct((B, S, D), jnp.float32),
        compiler_params=pltpu.CompilerParams(
            dimension_semantics=("arbitrary", "arbitrary"), vmem_limit_bytes=VMEM_LIMIT),
        name="out_b",
    )(yo, qm, sg, x, km, vmh, wo)


def kernel(x, mem, positions, a_norm, a_w_in, a_q_norm, a_k_norm, a_lambda_q1, a_lambda_k1,
           a_lambda_q2, a_lambda_k2, a_subln, a_w_out, kv_norm, w_kv_shared, b_norm, b_w_in,
           b_w_out, mem_norm, mem_w_kv, mem_q_norm, mem_k_norm):
    B, S, D = x.shape
    dw = DIFF_HEADS * 2 * HEAD_DIM
    mw = MEM_HEADS * HEAD_DIM
    f32 = jnp.float32

    grp = jnp.arange(MXU_DIM) // HEAD_DIM
    gmat = _bf16(grp[:, None] == grp[None, :])
    tri = _bf16(jnp.triu(jnp.ones((TILE, TILE), f32)))
    d = jnp.arange(LANES) % HEAD_DIM
    inv_freq = ROPE_THETA ** (-jnp.arange(0, ROPE_DIM, 2, dtype=f32) / ROPE_DIM)
    inv = jnp.where(d < ROPE_DIM, inv_freq[d % (ROPE_DIM // 2)], 0.0).reshape(1, LANES)
    slo = jnp.where(d < ROPE_DIM // 2, -1.0, 0.0).astype(f32).reshape(1, LANES)
    shi = jnp.where((d >= ROPE_DIM // 2) & (d < ROPE_DIM), 1.0, 0.0).astype(f32).reshape(1, LANES)

    wa = a_w_in[0]
    regroup = lambda w: w.reshape(D, 2, DIFF_HEADS, HEAD_DIM).transpose(0, 2, 1, 3).reshape(D, dw)
    wq = _bf16(regroup(wa[:, :dw]))
    wk = _bf16(regroup(wa[:, dw:2 * dw]))
    wvt = _bf16(wa[:, 2 * dw:3 * dw].T)
    wqm = _bf16(wa[:, 3 * dw:3 * dw + mw])
    wg = _bf16(wa[:, 3 * dw + mw:])
    gq = (jnp.tile(a_q_norm[0], 2 * DIFF_HEADS) * SCALE).reshape(1, dw)
    gk = jnp.tile(a_k_norm[0], 2 * DIFF_HEADS).reshape(1, dw)
    gqm0 = (jnp.tile(mem_q_norm[0], MEM_HEADS) * SCALE).reshape(1, mw)
    gqm1 = (jnp.tile(mem_q_norm[1], MEM_HEADS) * SCALE).reshape(1, mw)
    lambda_init = 0.8 - 0.6 * math.exp(-0.3 * 0)
    gsub = (a_subln[0] * (1.0 - lambda_init)).reshape(2 * HEAD_DIM, 1)

    km, vmh, lam = _mem_prep(mem, mem_norm, mem_w_kv, mem_k_norm, gmat,
                             a_lambda_q1, a_lambda_k1, a_lambda_q2, a_lambda_k2)

    pos = positions.astype(f32).reshape(B, S, 1)
    q0, k0, vt0, qm0, sg0 = _in_proj_a(x, pos, a_norm[0].reshape(1, D), wq, wk, wvt, wqm, wg,
                                       gq, gk, gqm0, inv, slo, shi, gmat)
    y0 = _diff_attn(lam, q0, k0, vt0, sg0, gsub)

    wb = b_w_in[0]
    x1, k1, vt1, q1, qm1, sg1 = _mid(
        y0, qm0, sg0, x, km, vmh, _bf16(a_w_out[0]), kv_norm.reshape(1, D),
        _bf16(w_kv_shared[:, :dw]), _bf16(w_kv_shared[:, dw:].T), b_norm[0].reshape(1, D),
        _bf16(wb[:, :dw] * SCALE), _bf16(wb[:, dw:dw + mw]), _bf16(wb[:, dw + mw:]), gqm1, gmat)
    y1 = _sb_attn(q1, k1, vt1, sg1, tri)
    return _out_b(y1, qm1, sg1, x1, km, vmh, _bf16(b_w_out[0]))
```

```python
import functools
import math

import jax
import jax.numpy as jnp
from jax import lax
from jax.experimental import pallas as pl
from jax.experimental.pallas import tpu as pltpu

HEAD_DIM = 64
DIFF_HEADS = 6
SB_HEADS = 12
MEM_HEADS = 4
ROPE_DIM = 16
ROPE_THETA = 500000.0
EPS = 1e-6
SCALE = HEAD_DIM ** -0.5
LOG2E = math.log2(math.e)

LANES = 128
MXU_DIM = 256
VMEM_LIMIT = 48 * 1024 * 1024

TILE = MXU_DIM
KEY_TILES_PER_STEP = 4
SB_KEY_TILES_PER_STEP = 2
SB_CUTOFF_LOG2 = 152.0


def _key_chunk(seq):
    return min(KEY_TILES_PER_STEP * TILE, seq)


def _bf16(x):
    return x.astype(jnp.bfloat16)


def _dot(a, b):
    return jnp.dot(a, b, preferred_element_type=jnp.float32)


def _dot_nt(a, b):
    return lax.dot_general(a, b, (((1,), (1,)), ((), ())), preferred_element_type=jnp.float32)


def _rms_scale(x):
    return x * lax.rsqrt(jnp.mean(x * x, axis=-1, keepdims=True) + EPS)


def _group_norm(p, gmat, gain):
    sq = p * p
    hi = _bf16(sq)
    lo = _bf16(sq - hi.astype(jnp.float32))
    ss = _dot(hi, gmat) + _dot(lo, gmat)
    return p * lax.rsqrt(ss * (1.0 / HEAD_DIM) + EPS) * gain


def _silu(g):
    return g / (1.0 + jnp.exp(-g))


def _mem_attention(qm, km, vmh_ref):
    head_of_lane = lax.broadcasted_iota(jnp.int32, qm.shape, 1) // HEAD_DIM
    om = None
    for h in range(MEM_HEADS):
        qh = jnp.where(head_of_lane == h, qm, jnp.zeros_like(qm))
        sc = _dot_nt(qh, km)
        m = jnp.max(sc, axis=1, keepdims=True)
        p = jnp.exp(sc - m)
        p = p * (1.0 / jnp.sum(p, axis=1, keepdims=True))
        o = _dot(_bf16(p), vmh_ref[h])
        om = o if om is None else om + o
    return om


def _mem_prep_kernel(mem_ref, nrm_ref, w_ref, gk_ref, gmat_ref, lq1_ref, lk1_ref, lq2_ref,
                     lk2_ref, km_ref, vmh_ref, lam_ref):
    h = _bf16(_rms_scale(mem_ref[0]) * nrm_ref[0])
    kv = _dot(h, w_ref[0])
    mw = MEM_HEADS * HEAD_DIM
    km_ref[0, 0] = _bf16(_group_norm(kv[:, :mw], gmat_ref[...], gk_ref[0]))
    vm = _bf16(kv[:, mw:])
    head_of_lane = lax.broadcasted_iota(jnp.int32, vm.shape, 1) // HEAD_DIM
    for hh in range(MEM_HEADS):
        vmh_ref[0, 0, hh] = jnp.where(head_of_lane == hh, vm, jnp.zeros_like(vm))
    lambda_init = 0.8 - 0.6 * math.exp(-0.3 * 0)
    s1 = jnp.sum(lq1_ref[...] * lk1_ref[...], axis=1, keepdims=True)
    s2 = jnp.sum(lq2_ref[...] * lk2_ref[...], axis=1, keepdims=True)
    lam = jnp.exp(s1) - jnp.exp(s2) + lambda_init
    lam_ref[...] = jnp.broadcast_to(lam, lam_ref.shape)


def _mem_prep(mem, mem_norm, mem_w_kv, mem_k_norm, gmat, lq1, lk1, lq2, lk2):
    B, M, D = mem.shape
    L = mem_norm.shape[0]
    mw = MEM_HEADS * HEAD_DIM
    gk = jnp.tile(mem_k_norm, (1, MEM_HEADS)).reshape(L, 1, mw)
    vec = lambda: pl.BlockSpec((1, HEAD_DIM), lambda l, b: (0, 0))
    return pl.pallas_call(
        _mem_prep_kernel,
        grid=(L, B),
        in_specs=[
            pl.BlockSpec((1, M, D), lambda l, b: (b, 0, 0)),
            pl.BlockSpec((1, 1, D), lambda l, b: (l, 0, 0)),
            pl.BlockSpec((1, D, 2 * mw), lambda l, b: (l, 0, 0)),
            pl.BlockSpec((1, 1, mw), lambda l, b: (l, 0, 0)),
            pl.BlockSpec((mw, mw), lambda l, b: (0, 0)),
            vec(), vec(), vec(), vec(),
        ],
        out_specs=[
            pl.BlockSpec((1, 1, M, mw), lambda l, b: (l, b, 0, 0)),
            pl.BlockSpec((1, 1, MEM_HEADS, M, mw), lambda l, b: (l, b, 0, 0, 0)),
            pl.BlockSpec((8, TILE), lambda l, b: (0, 0)),
        ],
        out_shape=[
            jax.ShapeDtypeStruct((L, B, M, mw), jnp.bfloat16),
            jax.ShapeDtypeStruct((L, B, MEM_HEADS, M, mw), jnp.bfloat16),
            jax.ShapeDtypeStruct((8, TILE), jnp.float32),
        ],
        compiler_params=pltpu.CompilerParams(
            dimension_semantics=("arbitrary", "arbitrary"), vmem_limit_bytes=VMEM_LIMIT),
        name="mem_prep",
    )(mem, mem_norm.reshape(L, 1, D), _bf16(mem_w_kv), gk, gmat, lq1, lk1, lq2, lk2)


def _in_proj_a_kernel(x_ref, pos_ref, nrm_ref, wq_ref, wk_ref, wvt_ref, wqm_ref, wg_ref,
                      gq_ref, gk_ref, gqm_ref, inv_ref, slo_ref, shi_ref, gmat_ref,
                      q_out, k_out, vt_out, qm_out, sg_out):
    h = _bf16(_rms_scale(x_ref[0]) * nrm_ref[...])
    gmat = gmat_ref[...]

    ang = pos_ref[0] * inv_ref[...]
    cos = jnp.cos(ang)
    sin = jnp.sin(ang)
    sin_lo = sin * slo_ref[...]
    sin_hi = sin * shi_ref[...]

    def norm_rope(w_ref, g_ref, out_ref):
        p = _dot(h, w_ref[...])
        for c in range(p.shape[1] // MXU_DIM):
            cs = slice(c * MXU_DIM, (c + 1) * MXU_DIM)
            y = _group_norm(p[:, cs], gmat, g_ref[:, cs])
            for half in range(MXU_DIM // LANES):
                yb = y[:, half * LANES:(half + 1) * LANES]
                y_next = pltpu.roll(yb, LANES - ROPE_DIM // 2, axis=1)
                y_prev = pltpu.roll(yb, ROPE_DIM // 2, axis=1)
                lo = c * MXU_DIM + half * LANES
                out_ref[0, :, lo:lo + LANES] = _bf16(yb * cos + y_next * sin_lo + y_prev * sin_hi)

    norm_rope(wq_ref, gq_ref, q_out)
    norm_rope(wk_ref, gk_ref, k_out)
    vt_out[0, 0] = _bf16(_dot_nt(wvt_ref[...], h))
    qm_out[0] = _bf16(_group_norm(_dot(h, wqm_ref[...]), gmat, gqm_ref[...]))
    sg_out[0] = _bf16(_silu(_dot(h, wg_ref[...])))


def _in_proj_a(x, pos, nrm, wq, wk, wvt, wqm, wg, gq, gk, gqm, inv, slo, shi, gmat):
    B, S, D = x.shape
    tpc = _key_chunk(S) // TILE
    dw = wq.shape[1]
    mw = wqm.shape[1]
    gw = wg.shape[1]
    full = lambda a: pl.BlockSpec(a.shape, lambda b, i: (0,) * a.ndim)
    row = lambda w: pl.BlockSpec((1, TILE, w), lambda b, i: (b, i, 0))
    consts = (nrm, wq, wk, wvt, wqm, wg, gq, gk, gqm, inv, slo, shi, gmat)
    return pl.pallas_call(
        _in_proj_a_kernel,
        grid=(B, S // TILE),
        in_specs=[row(D), row(1)] + [full(a) for a in consts],
        out_specs=[row(dw), row(dw),
                   pl.BlockSpec((1, 1, dw, TILE), lambda b, i: (b, i // tpc, 0, i % tpc)),
                   row(mw), row(gw)],
        out_shape=[
            jax.ShapeDtypeStruct((B, S, dw), jnp.bfloat16),
            jax.ShapeDtypeStruct((B, S, dw), jnp.bfloat16),
            jax.ShapeDtypeStruct((B, S // (tpc * TILE), dw, tpc * TILE), jnp.bfloat16),
            jax.ShapeDtypeStruct((B, S, mw), jnp.bfloat16),
            jax.ShapeDtypeStruct((B, S, gw), jnp.bfloat16),
        ],
        compiler_params=pltpu.CompilerParams(
            dimension_semantics=("arbitrary", "arbitrary"), vmem_limit_bytes=VMEM_LIMIT),
        name="in_proj_a",
    )(x, pos, *consts)


def _diff_attn_kernel(lam_ref, q_ref, k_ref, vt_ref, sg_ref, g_ref, o_ref,
                      acc_ref, s0_ref, s1_ref, ml_ref, *, kchunk):
    qi = pl.program_id(2)
    tiles_per_chunk = kchunk // TILE
    dc = qi // tiles_per_chunk
    q = q_ref[0]
    lane = lax.broadcasted_iota(jnp.int32, q.shape, 1)
    zero = jnp.zeros_like(q)
    qboth = jnp.concatenate(
        [jnp.where(lane < HEAD_DIM, q, zero), jnp.where(lane >= HEAD_DIM, q, zero)], axis=0)

    def scores_into(kc, dst_ref):
        kblk = k_ref[0, pl.ds(pl.multiple_of(kc * kchunk, kchunk), kchunk), :]
        dst_ref[...] = _dot_nt(kblk, qboth)

    scores_into(dc, s0_ref)
    scores_into(0, s1_ref)

    row = lax.broadcasted_iota(jnp.int32, (kchunk, 2 * TILE), 0)
    col = lax.broadcasted_iota(jnp.int32, (kchunk, 2 * TILE), 1) % TILE
    valid = row - col <= (qi - dc * tiles_per_chunk) * TILE
    s = jnp.where(valid, s0_ref[...], -jnp.inf)
    m = jnp.max(s, axis=0, keepdims=True)
    p = jnp.exp2(s - m)
    acc_ref[...] = _dot(vt_ref[0, dc], _bf16(p))
    ml_ref[0:1, :] = m
    ml_ref[1:2, :] = jnp.sum(p, axis=0, keepdims=True)

    def step(kc, cur_ref, next_ref):
        scores_into(jnp.minimum(kc + 1, dc - 1), next_ref)
        s = cur_ref[...]
        m_prev = ml_ref[0:1, :]
        m_new = jnp.maximum(m_prev, jnp.max(s, axis=0, keepdims=True))
        p = jnp.exp2(s - m_new)
        alpha = jnp.exp2(m_prev - m_new)
        acc_ref[...] = alpha * acc_ref[...] + _dot(vt_ref[0, kc], _bf16(p))
        ml_ref[0:1, :] = m_new
        ml_ref[1:2, :] = alpha * ml_ref[1:2, :] + jnp.sum(p, axis=0, keepdims=True)

    def body(kc, carry):
        @pl.when(kc % 2 == 0)
        def _():
            step(kc, s1_ref, s0_ref)

        @pl.when(kc % 2 == 1)
        def _():
            step(kc, s0_ref, s1_ref)

        return carry

    lax.fori_loop(0, dc, body, 0)

    lam = lam_ref[0:1, :]
    r = 1.0 / ml_ref[1:2, :]
    o = acc_ref[:, :TILE] * r[:, :TILE] - acc_ref[:, TILE:] * (lam * r[:, TILE:])
    o = o * lax.rsqrt(jnp.mean(o * o, axis=0, keepdims=True) + EPS) * g_ref[...]
    o_ref[0] = _bf16(o.T * sg_ref[0].astype(jnp.float32))


def _diff_attn(lam, q, k, vt, sg, gcol):
    B, S, dw = q.shape
    nt = S // TILE
    hw = 2 * HEAD_DIM
    nc, kchunk = vt.shape[1], vt.shape[3]
    return pl.pallas_call(
        functools.partial(_diff_attn_kernel, kchunk=kchunk),
        grid=(B, dw // hw, nt),
        in_specs=[
            pl.BlockSpec((8, TILE), lambda b, h, i: (0, 0)),
            pl.BlockSpec((1, TILE, hw), lambda b, h, i: (b, i, h)),
            pl.BlockSpec((1, S, hw), lambda b, h, i: (b, 0, h)),
            pl.BlockSpec((1, nc, hw, kchunk), lambda b, h, i: (b, 0, h, 0)),
            pl.BlockSpec((1, TILE, hw), lambda b, h, i: (b, i, h)),
            pl.BlockSpec((hw, 1), lambda b, h, i: (0, 0)),
        ],
        out_specs=pl.BlockSpec((1, TILE, hw), lambda b, h, i: (b, i, h)),
        out_shape=jax.ShapeDtypeStruct((B, S, dw), jnp.bfloat16),
        scratch_shapes=[pltpu.VMEM((hw, 2 * TILE), jnp.float32),
                        pltpu.VMEM((kchunk, 2 * TILE), jnp.float32),
                        pltpu.VMEM((kchunk, 2 * TILE), jnp.float32),
                        pltpu.VMEM((8, 2 * TILE), jnp.float32)],
        compiler_params=pltpu.CompilerParams(
            dimension_semantics=("arbitrary", "arbitrary", "arbitrary"),
            vmem_limit_bytes=VMEM_LIMIT),
        name="diff_attn",
    )(lam, q, k, vt, sg, gcol)


def _mid_kernel(yo_ref, qm_ref, sgm_ref, x_ref, km_ref, vmh_ref, wo_ref, gkv_ref, wk_ref,
                wvt_ref, gb_ref, wq_ref, wqm_ref, wg_ref, gqm_ref, gmat_ref,
                x1_out, k_out, vt_out, q_out, qm_out, sg_out):
    dw = yo_ref.shape[2]
    om = _mem_attention(qm_ref[0], km_ref[0, 0], vmh_ref.at[0, 0])
    ym = _bf16(om * sgm_ref[0].astype(jnp.float32))
    x1 = x_ref[0] + _dot(yo_ref[0], wo_ref[:dw, :]) + _dot(ym, wo_ref[dw:, :])
    x1_out[0] = x1
    xn = _rms_scale(x1)
    hk = _bf16(xn * gkv_ref[...])
    k_out[0] = _bf16(_dot(hk, wk_ref[...]))
    vt_out[0, 0] = _bf16(_dot_nt(wvt_ref[...], hk))
    hb = _bf16(xn * gb_ref[...])
    q_out[0] = _bf16(_dot(hb, wq_ref[...]))
    qm_out[0] = _bf16(_group_norm(_dot(hb, wqm_ref[...]), gmat_ref[...], gqm_ref[...]))
    sg_out[0] = _bf16(_silu(_dot(hb, wg_ref[...])))


def _mid(yo, qm, sg, x, km, vmh, wo, gkv, wk, wvt, gb, wq, wqm, wg, gqm, gmat):
    B, S, D = x.shape
    dw = yo.shape[2]
    mw = qm.shape[2]
    gw = wg.shape[1]
    M = km.shape[2]
    full = lambda a: pl.BlockSpec(a.shape, lambda b, i: (0,) * a.ndim)
    row = lambda w: pl.BlockSpec((1, TILE, w), lambda b, i: (b, i, 0))
    consts = (wo, gkv, wk, wvt, gb, wq, wqm, wg, gqm, gmat)
    return pl.pallas_call(
        _mid_kernel,
        grid=(B, S // TILE),
        in_specs=[row(dw), row(mw),
                  pl.BlockSpec((1, TILE, mw), lambda b, i: (b, i, dw // mw)),
                  row(D),
                  pl.BlockSpec((1, 1, M, mw), lambda b, i: (0, b, 0, 0)),
                  pl.BlockSpec((1, 1, MEM_HEADS, M, mw), lambda b, i: (0, b, 0, 0, 0)),
                  ] + [full(a) for a in consts],
        out_specs=[row(D), row(dw),
                   pl.BlockSpec((1, 1, dw, TILE), lambda b, i: (b, i, 0, 0)),
                   row(dw), row(mw), row(gw)],
        out_shape=[
            jax.ShapeDtypeStruct((B, S, D), jnp.float32),
            jax.ShapeDtypeStruct((B, S, dw), jnp.bfloat16),
            jax.ShapeDtypeStruct((B, S // TILE, dw, TILE), jnp.bfloat16),
            jax.ShapeDtypeStruct((B, S, dw), jnp.bfloat16),
            jax.ShapeDtypeStruct((B, S, mw), jnp.bfloat16),
            jax.ShapeDtypeStruct((B, S, gw), jnp.bfloat16),
        ],
        compiler_params=pltpu.CompilerParams(
            dimension_semantics=("arbitrary", "arbitrary"), vmem_limit_bytes=VMEM_LIMIT),
        name="mid",
    )(yo, qm, sg, x, km, vmh, *consts)


def _sb_attn_kernel(q_ref, k_ref, vt_ref, sg_ref, tri_ref, o_ref, acc_ref):
    qi = pl.program_id(2)
    q = q_ref[0]
    lane = lax.broadcasted_iota(jnp.int32, q.shape, 1)
    zero = jnp.zeros_like(q)
    qboth = jnp.concatenate(
        [jnp.where(lane < HEAD_DIM, q, zero), jnp.where(lane >= HEAD_DIM, q, zero)], axis=0)
    tri = tri_ref[...]

    def window(t0, ntiles, right, valid, first):
        kblk = k_ref[0, pl.ds(pl.multiple_of(t0 * TILE, TILE), ntiles * TILE), :]
        z = _dot_nt(kblk, qboth)
        sp = jnp.maximum(z, 0.0) + jnp.log2(1.0 + jnp.exp2(-jnp.abs(z)))
        if valid is not None:
            sp = jnp.where(valid, sp, 0.0)
        hi = _bf16(sp)
        lo = _bf16(sp - hi.astype(jnp.float32))
        contrib = None
        for j in reversed(range(ntiles)):
            ks = slice(j * TILE, (j + 1) * TILE)
            csum = _dot(tri, hi[ks]) + _dot(tri, lo[ks])
            a = jnp.exp2(z[ks] - csum - right)
            if valid is not None:
                a = jnp.where(valid[ks], a, 0.0)
            o = _dot(vt_ref[0, t0 + j], _bf16(a))
            contrib = o if contrib is None else contrib + o
            right = right + csum[0:1, :]
        if first:
            acc_ref[...] = contrib
        else:
            acc_ref[...] += contrib
        return right

    def unfinished(right):
        smallest = jnp.min(right, axis=1, keepdims=True)
        return (smallest[0, 0] <= SB_CUTOFF_LOG2).astype(jnp.int32)

    nwin = SB_KEY_TILES_PER_STEP
    t0 = jnp.maximum(qi - (nwin - 1), 0)
    row = lax.broadcasted_iota(jnp.int32, (nwin * TILE, 2 * TILE), 0)
    col = lax.broadcasted_iota(jnp.int32, (nwin * TILE, 2 * TILE), 1) % TILE
    valid = row - col < (qi - t0) * TILE
    right = window(t0, nwin, jnp.zeros((1, 2 * TILE), jnp.float32), valid, True)

    def cond(state):
        return jnp.logical_and(state[0] >= nwin, state[1] > 0)

    def body(state):
        t = state[0] - nwin
        new = window(t, nwin, state[2], None, False)
        return (t, unfinished(new), new)

    t, more, right = lax.while_loop(cond, body, (t0, unfinished(right), right))

    for rem in range(1, nwin):
        @pl.when(jnp.logical_and(t == rem, more > 0))
        def _():
            window(0, rem, right, None, False)

    acc = acc_ref[...]
    o = jnp.concatenate([acc[:HEAD_DIM, :TILE], acc[HEAD_DIM:, TILE:]], axis=0)
    o_ref[0] = _bf16(o.T * sg_ref[0].astype(jnp.float32))


def _sb_attn(q, k, vt, sg, tri):
    B, S, dw = q.shape
    nt = S // TILE
    hw = 2 * HEAD_DIM
    return pl.pallas_call(
        _sb_attn_kernel,
        grid=(B, dw // hw, nt),
        in_specs=[
            pl.BlockSpec((1, TILE, hw), lambda b, h, i: (b, i, h)),
            pl.BlockSpec((1, S, hw), lambda b, h, i: (b, 0, h)),
            pl.BlockSpec((1, nt, hw, TILE), lambda b, h, i: (b, 0, h, 0)),
            pl.BlockSpec((1, TILE, hw), lambda b, h, i: (b, i, h)),
            pl.BlockSpec((TILE, TILE), lambda b, h, i: (0, 0)),
        ],
        out_specs=pl.BlockSpec((1, TILE, hw), lambda b, h, i: (b, i, h)),
        out_shape=jax.ShapeDtypeStruct((B, S, dw), jnp.bfloat16),
        scratch_shapes=[pltpu.VMEM((hw, 2 * TILE), jnp.float32)],
        compiler_params=pltpu.CompilerParams(
            dimension_semantics=("arbitrary", "arbitrary", "arbitrary"),
            vmem_limit_bytes=VMEM_LIMIT),
        name="sb_attn",
    )(q, k, vt, sg, tri)


def _out_b_kernel(yo_ref, qm_ref, sgm_ref, x_ref, km_ref, vmh_ref, wo_ref, out_ref):
    dw = yo_ref.shape[2]
    om = _mem_attention(qm_ref[0], km_ref[0, 0], vmh_ref.at[0, 0])
    ym = _bf16(om * sgm_ref[0].astype(jnp.float32))
    out_ref[0] = x_ref[0] + _dot(yo_ref[0], wo_ref[:dw, :]) + _dot(ym, wo_ref[dw:, :])


def _out_b(yo, qm, sg, x, km, vmh, wo):
    B, S, D = x.shape
    dw = yo.shape[2]
    mw = qm.shape[2]
    M = km.shape[2]
    row = lambda w: pl.BlockSpec((1, TILE, w), lambda b, i: (b, i, 0))
    return pl.pallas_call(
        _out_b_kernel,
        grid=(B, S // TILE),
        in_specs=[row(dw), row(mw),
                  pl.BlockSpec((1, TILE, mw), lambda b, i: (b, i, dw // mw)),
                  row(D),
                  pl.BlockSpec((1, 1, M, mw), lambda b, i: (1, b, 0, 0)),
                  pl.BlockSpec((1, 1, MEM_HEADS, M, mw), lambda b, i: (1, b, 0, 0, 0)),
                  pl.BlockSpec(wo.shape, lambda b, i: (0, 0))],
        out_specs=row(D),
        out_shape=jax.ShapeDtypeStruct((B, S, D), jnp.float32),
        compiler_params=pltpu.CompilerParams(
            dimension_semantics=("arbitrary", "arbitrary"), vmem_limit_bytes=VMEM_LIMIT),
        name="out_b",
    )(yo, qm, sg, x, km, vmh, wo)


def kernel(x, mem, positions, a_norm, a_w_in, a_q_norm, a_k_norm, a_lambda_q1, a_lambda_k1,
           a_lambda_q2, a_lambda_k2, a_subln, a_w_out, kv_norm, w_kv_shared, b_norm, b_w_in,
           b_w_out, mem_norm, mem_w_kv, mem_q_norm, mem_k_norm):
    B, S, D = x.shape
    dw = DIFF_HEADS * 2 * HEAD_DIM
    mw = MEM_HEADS * HEAD_DIM
    f32 = jnp.float32

    grp = jnp.arange(MXU_DIM) // HEAD_DIM
    gmat = _bf16(grp[:, None] == grp[None, :])
    tri = _bf16(jnp.triu(jnp.ones((TILE, TILE), f32)))
    d = jnp.arange(LANES) % HEAD_DIM
    inv_freq = ROPE_THETA ** (-jnp.arange(0, ROPE_DIM, 2, dtype=f32) / ROPE_DIM)
    inv = jnp.where(d < ROPE_DIM, inv_freq[d % (ROPE_DIM // 2)], 0.0).reshape(1, LANES)
    slo = jnp.where(d < ROPE_DIM // 2, -1.0, 0.0).astype(f32).reshape(1, LANES)
    shi = jnp.where((d >= ROPE_DIM // 2) & (d < ROPE_DIM), 1.0, 0.0).astype(f32).reshape(1, LANES)

    wa = a_w_in[0]
    regroup = lambda w: w.reshape(D, 2, DIFF_HEADS, HEAD_DIM).transpose(0, 2, 1, 3).reshape(D, dw)
    wq = _bf16(regroup(wa[:, :dw]))
    wk = _bf16(regroup(wa[:, dw:2 * dw]))
    wvt = _bf16(wa[:, 2 * dw:3 * dw].T)
    wqm = _bf16(wa[:, 3 * dw:3 * dw + mw])
    wg = _bf16(wa[:, 3 * dw + mw:])
    gq = (jnp.tile(a_q_norm[0], 2 * DIFF_HEADS) * (SCALE * LOG2E)).reshape(1, dw)
    gk = jnp.tile(a_k_norm[0], 2 * DIFF_HEADS).reshape(1, dw)
    gqm0 = (jnp.tile(mem_q_norm[0], MEM_HEADS) * SCALE).reshape(1, mw)
    gqm1 = (jnp.tile(mem_q_norm[1], MEM_HEADS) * SCALE).reshape(1, mw)
    lambda_init = 0.8 - 0.6 * math.exp(-0.3 * 0)
    gsub = (a_subln[0] * (1.0 - lambda_init)).reshape(2 * HEAD_DIM, 1)

    km, vmh, lam = _mem_prep(mem, mem_norm, mem_w_kv, mem_k_norm, gmat,
                             a_lambda_q1, a_lambda_k1, a_lambda_q2, a_lambda_k2)

    pos = positions.astype(f32).reshape(B, S, 1)
    q0, k0, vt0, qm0, sg0 = _in_proj_a(x, pos, a_norm[0].reshape(1, D), wq, wk, wvt, wqm, wg,
                                       gq, gk, gqm0, inv, slo, shi, gmat)
    y0 = _diff_attn(lam, q0, k0, vt0, sg0, gsub)

    wb = b_w_in[0]
    x1, k1, vt1, q1, qm1, sg1 = _mid(
        y0, qm0, sg0, x, km, vmh, _bf16(a_w_out[0]), kv_norm.reshape(1, D),
        _bf16(w_kv_shared[:, :dw]), _bf16(w_kv_shared[:, dw:].T), b_norm[0].reshape(1, D),
        _bf16(wb[:, :dw] * (SCALE * LOG2E)), _bf16(wb[:, dw:dw + mw]), _bf16(wb[:, dw + mw:]),
        gqm1, gmat)
    y1 = _sb_attn(q1, k1, vt1, sg1, tri)
    return _out_b(y1, qm1, sg1, x1, km, vmh, _bf16(b_w_out[0]))
```

```python
import functools
import math

import jax
import jax.numpy as jnp
from jax import lax
from jax.experimental import pallas as pl
from jax.experimental.pallas import tpu as pltpu

HEAD_DIM = 64
DIFF_HEADS = 6
SB_HEADS = 12
MEM_HEADS = 4
ROPE_DIM = 16
ROPE_THETA = 500000.0
EPS = 1e-6
SCALE = HEAD_DIM ** -0.5
LOG2E = math.log2(math.e)

LANES = 128
MXU_DIM = 256
VMEM_LIMIT = 48 * 1024 * 1024

TILE = MXU_DIM
ROWS = 2 * TILE
KEY_TILES_PER_STEP = 4
SB_KEY_TILES_PER_STEP = 2
SB_CUTOFF_LOG2 = 152.0


def _key_chunk(seq):
    return min(KEY_TILES_PER_STEP * TILE, seq)


def _bf16(x):
    return x.astype(jnp.bfloat16)


def _dot(a, b):
    return jnp.dot(a, b, preferred_element_type=jnp.float32)


def _dot_nt(a, b):
    return lax.dot_general(a, b, (((1,), (1,)), ((), ())), preferred_element_type=jnp.float32)


def _rms_scale(x):
    return x * lax.rsqrt(jnp.mean(x * x, axis=-1, keepdims=True) + EPS)


def _group_norm(p, gmat, gain):
    sq = p * p
    hi = _bf16(sq)
    lo = _bf16(sq - hi.astype(jnp.float32))
    ss = _dot(hi, gmat) + _dot(lo, gmat)
    return p * lax.rsqrt(ss * (1.0 / HEAD_DIM) + EPS) * gain


def _silu(g):
    return g / (1.0 + jnp.exp(-g))


def _mem_attention(qm, km, vmh_ref):
    head_of_lane = lax.broadcasted_iota(jnp.int32, qm.shape, 1) // HEAD_DIM
    om = None
    for h in range(MEM_HEADS):
        qh = jnp.where(head_of_lane == h, qm, jnp.zeros_like(qm))
        sc = _dot_nt(qh, km)
        m = jnp.max(sc, axis=1, keepdims=True)
        p = jnp.exp(sc - m)
        p = p * (1.0 / jnp.sum(p, axis=1, keepdims=True))
        o = _dot(_bf16(p), vmh_ref[h])
        om = o if om is None else om + o
    return om


def _mem_prep_kernel(mem_ref, nrm_ref, w_ref, gk_ref, gmat_ref, lq1_ref, lk1_ref, lq2_ref,
                     lk2_ref, km_ref, vmh_ref, lam_ref):
    h = _bf16(_rms_scale(mem_ref[0]) * nrm_ref[0])
    kv = _dot(h, w_ref[0])
    mw = MEM_HEADS * HEAD_DIM
    km_ref[0, 0] = _bf16(_group_norm(kv[:, :mw], gmat_ref[...], gk_ref[0]))
    vm = _bf16(kv[:, mw:])
    head_of_lane = lax.broadcasted_iota(jnp.int32, vm.shape, 1) // HEAD_DIM
    for hh in range(MEM_HEADS):
        vmh_ref[0, 0, hh] = jnp.where(head_of_lane == hh, vm, jnp.zeros_like(vm))
    lambda_init = 0.8 - 0.6 * math.exp(-0.3 * 0)
    s1 = jnp.sum(lq1_ref[...] * lk1_ref[...], axis=1, keepdims=True)
    s2 = jnp.sum(lq2_ref[...] * lk2_ref[...], axis=1, keepdims=True)
    lam = jnp.exp(s1) - jnp.exp(s2) + lambda_init
    lam_ref[...] = jnp.broadcast_to(lam, lam_ref.shape)


def _mem_prep(mem, mem_norm, mem_w_kv, mem_k_norm, gmat, lq1, lk1, lq2, lk2):
    B, M, D = mem.shape
    L = mem_norm.shape[0]
    mw = MEM_HEADS * HEAD_DIM
    gk = jnp.tile(mem_k_norm, (1, MEM_HEADS)).reshape(L, 1, mw)
    vec = lambda: pl.BlockSpec((1, HEAD_DIM), lambda l, b: (0, 0))
    return pl.pallas_call(
        _mem_prep_kernel,
        grid=(L, B),
        in_specs=[
            pl.BlockSpec((1, M, D), lambda l, b: (b, 0, 0)),
            pl.BlockSpec((1, 1, D), lambda l, b: (l, 0, 0)),
            pl.BlockSpec((1, D, 2 * mw), lambda l, b: (l, 0, 0)),
            pl.BlockSpec((1, 1, mw), lambda l, b: (l, 0, 0)),
            pl.BlockSpec((mw, mw), lambda l, b: (0, 0)),
            vec(), vec(), vec(), vec(),
        ],
        out_specs=[
            pl.BlockSpec((1, 1, M, mw), lambda l, b: (l, b, 0, 0)),
            pl.BlockSpec((1, 1, MEM_HEADS, M, mw), lambda l, b: (l, b, 0, 0, 0)),
            pl.BlockSpec((8, TILE), lambda l, b: (0, 0)),
        ],
        out_shape=[
            jax.ShapeDtypeStruct((L, B, M, mw), jnp.bfloat16),
            jax.ShapeDtypeStruct((L, B, MEM_HEADS, M, mw), jnp.bfloat16),
            jax.ShapeDtypeStruct((8, TILE), jnp.float32),
        ],
        compiler_params=pltpu.CompilerParams(
            dimension_semantics=("arbitrary", "arbitrary"), vmem_limit_bytes=VMEM_LIMIT),
        name="mem_prep",
    )(mem, mem_norm.reshape(L, 1, D), _bf16(mem_w_kv), gk, gmat, lq1, lk1, lq2, lk2)


def _in_proj_a_kernel(x_ref, pos_ref, nrm_ref, wq_ref, wk_ref, wvt_ref, wqm_ref, wg_ref,
                      gq_ref, gk_ref, gqm_ref, inv_ref, slo_ref, shi_ref, gmat_ref,
                      q_out, k_out, vt_out, qm_out, sg_out):
    h = _bf16(_rms_scale(x_ref[0]) * nrm_ref[...])
    gmat = gmat_ref[...]

    ang = pos_ref[0] * inv_ref[...]
    cos = jnp.cos(ang)
    sin = jnp.sin(ang)
    sin_lo = sin * slo_ref[...]
    sin_hi = sin * shi_ref[...]

    def norm_rope(w_ref, g_ref, out_ref):
        p = _dot(h, w_ref[...])
        for c in range(p.shape[1] // MXU_DIM):
            cs = slice(c * MXU_DIM, (c + 1) * MXU_DIM)
            y = _group_norm(p[:, cs], gmat, g_ref[:, cs])
            for half in range(MXU_DIM // LANES):
                yb = y[:, half * LANES:(half + 1) * LANES]
                y_next = pltpu.roll(yb, LANES - ROPE_DIM // 2, axis=1)
                y_prev = pltpu.roll(yb, ROPE_DIM // 2, axis=1)
                lo = c * MXU_DIM + half * LANES
                out_ref[0, :, lo:lo + LANES] = _bf16(yb * cos + y_next * sin_lo + y_prev * sin_hi)

    norm_rope(wq_ref, gq_ref, q_out)
    norm_rope(wk_ref, gk_ref, k_out)
    vt_out[0, 0] = _bf16(_dot_nt(wvt_ref[...], h))
    qm_out[0] = _bf16(_group_norm(_dot(h, wqm_ref[...]), gmat, gqm_ref[...]))
    sg_out[0] = _bf16(_silu(_dot(h, wg_ref[...])))


def _in_proj_a(x, pos, nrm, wq, wk, wvt, wqm, wg, gq, gk, gqm, inv, slo, shi, gmat):
    B, S, D = x.shape
    rpc = _key_chunk(S) // ROWS
    dw = wq.shape[1]
    mw = wqm.shape[1]
    gw = wg.shape[1]
    full = lambda a: pl.BlockSpec(a.shape, lambda b, i: (0,) * a.ndim)
    row = lambda w: pl.BlockSpec((1, ROWS, w), lambda b, i: (b, i, 0))
    consts = (nrm, wq, wk, wvt, wqm, wg, gq, gk, gqm, inv, slo, shi, gmat)
    return pl.pallas_call(
        _in_proj_a_kernel,
        grid=(B, S // ROWS),
        in_specs=[row(D), row(1)] + [full(a) for a in consts],
        out_specs=[row(dw), row(dw),
                   pl.BlockSpec((1, 1, dw, ROWS), lambda b, i: (b, i // rpc, 0, i % rpc)),
                   row(mw), row(gw)],
        out_shape=[
            jax.ShapeDtypeStruct((B, S, dw), jnp.bfloat16),
            jax.ShapeDtypeStruct((B, S, dw), jnp.bfloat16),
            jax.ShapeDtypeStruct((B, S // (rpc * ROWS), dw, rpc * ROWS), jnp.bfloat16),
            jax.ShapeDtypeStruct((B, S, mw), jnp.bfloat16),
            jax.ShapeDtypeStruct((B, S, gw), jnp.bfloat16),
        ],
        compiler_params=pltpu.CompilerParams(
            dimension_semantics=("arbitrary", "arbitrary"), vmem_limit_bytes=VMEM_LIMIT),
        name="in_proj_a",
    )(x, pos, *consts)


def _diff_attn_kernel(lam_ref, q_ref, k_ref, vt_ref, sg_ref, g_ref, o_ref,
                      acc_ref, s0_ref, s1_ref, ml_ref, *, kchunk):
    qi = pl.program_id(2)
    tiles_per_chunk = kchunk // TILE
    dc = qi // tiles_per_chunk
    q = q_ref[0]
    lane = lax.broadcasted_iota(jnp.int32, q.shape, 1)
    zero = jnp.zeros_like(q)
    qboth = jnp.concatenate(
        [jnp.where(lane < HEAD_DIM, q, zero), jnp.where(lane >= HEAD_DIM, q, zero)], axis=0)

    def scores_into(kc, dst_ref):
        kblk = k_ref[0, pl.ds(pl.multiple_of(kc * kchunk, kchunk), kchunk), :]
        dst_ref[...] = _dot_nt(kblk, qboth)

    diag_tile = qi - dc * tiles_per_chunk
    for ntiles in range(1, tiles_per_chunk + 1):
        @pl.when(diag_tile == ntiles - 1)
        def _():
            scores_into(0, s1_ref)
            nkeys = ntiles * TILE
            kblk = k_ref[0, pl.ds(pl.multiple_of(dc * kchunk, kchunk), nkeys), :]
            s = _dot_nt(kblk, qboth)
            row = lax.broadcasted_iota(jnp.int32, s.shape, 0)
            col = lax.broadcasted_iota(jnp.int32, s.shape, 1) % TILE
            s = jnp.where(row - col <= (ntiles - 1) * TILE, s, -jnp.inf)
            m = jnp.max(s, axis=0, keepdims=True)
            p = jnp.exp2(s - m)
            acc_ref[...] = _dot(vt_ref[0, dc, :, :nkeys], _bf16(p))
            ml_ref[0:1, :] = m
            ml_ref[1:2, :] = jnp.sum(p, axis=0, keepdims=True)

    def step(kc, cur_ref, next_ref):
        scores_into(jnp.minimum(kc + 1, dc - 1), next_ref)
        s = cur_ref[...]
        m_prev = ml_ref[0:1, :]
        m_new = jnp.maximum(m_prev, jnp.max(s, axis=0, keepdims=True))
        p = jnp.exp2(s - m_new)
        alpha = jnp.exp2(m_prev - m_new)
        acc_ref[...] = alpha * acc_ref[...] + _dot(vt_ref[0, kc], _bf16(p))
        ml_ref[0:1, :] = m_new
        ml_ref[1:2, :] = alpha * ml_ref[1:2, :] + jnp.sum(p, axis=0, keepdims=True)

    def body(kc, carry):
        @pl.when(kc % 2 == 0)
        def _():
            step(kc, s1_ref, s0_ref)

        @pl.when(kc % 2 == 1)
        def _():
            step(kc, s0_ref, s1_ref)

        return carry

    lax.fori_loop(0, dc, body, 0)

    lam = lam_ref[0:1, :]
    r = 1.0 / ml_ref[1:2, :]
    o = acc_ref[:, :TILE] * r[:, :TILE] - acc_ref[:, TILE:] * (lam * r[:, TILE:])
    o = o * lax.rsqrt(jnp.mean(o * o, axis=0, keepdims=True) + EPS) * g_ref[...]
    o_ref[0] = _bf16(o.T * sg_ref[0].astype(jnp.float32))


def _diff_attn(lam, q, k, vt, sg, gcol):
    B, S, dw = q.shape
    nt = S // TILE
    hw = 2 * HEAD_DIM
    nc, kchunk = vt.shape[1], vt.shape[3]
    return pl.pallas_call(
        functools.partial(_diff_attn_kernel, kchunk=kchunk),
        grid=(B, dw // hw, nt),
        in_specs=[
            pl.BlockSpec((8, TILE), lambda b, h, i: (0, 0)),
            pl.BlockSpec((1, TILE, hw), lambda b, h, i: (b, i, h)),
            pl.BlockSpec((1, S, hw), lambda b, h, i: (b, 0, h)),
            pl.BlockSpec((1, nc, hw, kchunk), lambda b, h, i: (b, 0, h, 0)),
            pl.BlockSpec((1, TILE, hw), lambda b, h, i: (b, i, h)),
            pl.BlockSpec((hw, 1), lambda b, h, i: (0, 0)),
        ],
        out_specs=pl.BlockSpec((1, TILE, hw), lambda b, h, i: (b, i, h)),
        out_shape=jax.ShapeDtypeStruct((B, S, dw), jnp.bfloat16),
        scratch_shapes=[pltpu.VMEM((hw, 2 * TILE), jnp.float32),
                        pltpu.VMEM((kchunk, 2 * TILE), jnp.float32),
                        pltpu.VMEM((kchunk, 2 * TILE), jnp.float32),
                        pltpu.VMEM((8, 2 * TILE), jnp.float32)],
        compiler_params=pltpu.CompilerParams(
            dimension_semantics=("arbitrary", "arbitrary", "arbitrary"),
            vmem_limit_bytes=VMEM_LIMIT),
        name="diff_attn",
    )(lam, q, k, vt, sg, gcol)


def _mid_kernel(yo_ref, qm_ref, sgm_ref, x_ref, km_ref, vmh_ref, wo_ref, gkv_ref, wk_ref,
                wvt_ref, gb_ref, wq_ref, wqm_ref, wg_ref, gqm_ref, gmat_ref,
                x1_out, k_out, vt_out, q_out, qm_out, sg_out):
    dw = yo_ref.shape[2]
    om = _mem_attention(qm_ref[0], km_ref[0, 0], vmh_ref.at[0, 0])
    ym = _bf16(om * sgm_ref[0].astype(jnp.float32))
    x1 = x_ref[0] + _dot(yo_ref[0], wo_ref[:dw, :]) + _dot(ym, wo_ref[dw:, :])
    x1_out[0] = x1
    xn = _rms_scale(x1)
    hk = _bf16(xn * gkv_ref[...])
    k_out[0] = _bf16(_dot(hk, wk_ref[...]))
    vt = _bf16(_dot_nt(wvt_ref[...], hk))
    for j in range(vt_out.shape[1]):
        vt_out[0, j] = vt[:, j * TILE:(j + 1) * TILE]
    hb = _bf16(xn * gb_ref[...])
    q_out[0] = _bf16(_dot(hb, wq_ref[...]))
    qm_out[0] = _bf16(_group_norm(_dot(hb, wqm_ref[...]), gmat_ref[...], gqm_ref[...]))
    sg_out[0] = _bf16(_silu(_dot(hb, wg_ref[...])))


def _mid(yo, qm, sg, x, km, vmh, wo, gkv, wk, wvt, gb, wq, wqm, wg, gqm, gmat):
    B, S, D = x.shape
    dw = yo.shape[2]
    mw = qm.shape[2]
    gw = wg.shape[1]
    M = km.shape[2]
    full = lambda a: pl.BlockSpec(a.shape, lambda b, i: (0,) * a.ndim)
    row = lambda w: pl.BlockSpec((1, ROWS, w), lambda b, i: (b, i, 0))
    consts = (wo, gkv, wk, wvt, gb, wq, wqm, wg, gqm, gmat)
    return pl.pallas_call(
        _mid_kernel,
        grid=(B, S // ROWS),
        in_specs=[row(dw), row(mw),
                  pl.BlockSpec((1, ROWS, mw), lambda b, i: (b, i, dw // mw)),
                  row(D),
                  pl.BlockSpec((1, 1, M, mw), lambda b, i: (0, b, 0, 0)),
                  pl.BlockSpec((1, 1, MEM_HEADS, M, mw), lambda b, i: (0, b, 0, 0, 0)),
                  ] + [full(a) for a in consts],
        out_specs=[row(D), row(dw),
                   pl.BlockSpec((1, ROWS // TILE, dw, TILE), lambda b, i: (b, i, 0, 0)),
                   row(dw), row(mw), row(gw)],
        out_shape=[
            jax.ShapeDtypeStruct((B, S, D), jnp.float32),
            jax.ShapeDtypeStruct((B, S, dw), jnp.bfloat16),
            jax.ShapeDtypeStruct((B, S // TILE, dw, TILE), jnp.bfloat16),
            jax.ShapeDtypeStruct((B, S, dw), jnp.bfloat16),
            jax.ShapeDtypeStruct((B, S, mw), jnp.bfloat16),
            jax.ShapeDtypeStruct((B, S, gw), jnp.bfloat16),
        ],
        compiler_params=pltpu.CompilerParams(
            dimension_semantics=("arbitrary", "arbitrary"), vmem_limit_bytes=VMEM_LIMIT),
        name="mid",
    )(yo, qm, sg, x, km, vmh, *consts)


def _sb_attn_kernel(q_ref, k_ref, vt_ref, sg_ref, tri_ref, o_ref, acc_ref):
    qi = pl.program_id(2)
    q = q_ref[0]
    lane = lax.broadcasted_iota(jnp.int32, q.shape, 1)
    zero = jnp.zeros_like(q)
    qboth = jnp.concatenate(
        [jnp.where(lane < HEAD_DIM, q, zero), jnp.where(lane >= HEAD_DIM, q, zero)], axis=0)
    tri = tri_ref[...]

    def window(t0, ntiles, right, valid, first):
        kblk = k_ref[0, pl.ds(pl.multiple_of(t0 * TILE, TILE), ntiles * TILE), :]
        z = _dot_nt(kblk, qboth)
        sp = jnp.maximum(z, 0.0) + jnp.log2(1.0 + jnp.exp2(-jnp.abs(z)))
        if valid is not None:
            sp = jnp.where(valid, sp, 0.0)
        hi = _bf16(sp)
        lo = _bf16(sp - hi.astype(jnp.float32))
        contrib = None
        for j in reversed(range(ntiles)):
            ks = slice(j * TILE, (j + 1) * TILE)
            csum = _dot(tri, hi[ks]) + _dot(tri, lo[ks])
            a = jnp.exp2(z[ks] - csum - right)
            if valid is not None:
                a = jnp.where(valid[ks], a, 0.0)
            o = _dot(vt_ref[0, t0 + j], _bf16(a))
            contrib = o if contrib is None else contrib + o
            right = right + csum[0:1, :]
        if first:
            acc_ref[...] = contrib
        else:
            acc_ref[...] += contrib
        return right

    def unfinished(right):
        smallest = jnp.min(right, axis=1, keepdims=True)
        return (smallest[0, 0] <= SB_CUTOFF_LOG2).astype(jnp.int32)

    nwin = SB_KEY_TILES_PER_STEP
    t0 = jnp.maximum(qi - (nwin - 1), 0)
    row = lax.broadcasted_iota(jnp.int32, (nwin * TILE, 2 * TILE), 0)
    col = lax.broadcasted_iota(jnp.int32, (nwin * TILE, 2 * TILE), 1) % TILE
    valid = row - col < (qi - t0) * TILE
    right = window(t0, nwin, jnp.zeros((1, 2 * TILE), jnp.float32), valid, True)

    def cond(state):
        return jnp.logical_and(state[0] >= nwin, state[1] > 0)

    def body(state):
        t = state[0] - nwin
        new = window(t, nwin, state[2], None, False)
        return (t, unfinished(new), new)

    t, more, right = lax.while_loop(cond, body, (t0, unfinished(right), right))

    for rem in range(1, nwin):
        @pl.when(jnp.logical_and(t == rem, more > 0))
        def _():
            window(0, rem, right, None, False)

    acc = acc_ref[...]
    o = jnp.concatenate([acc[:HEAD_DIM, :TILE], acc[HEAD_DIM:, TILE:]], axis=0)
    o_ref[0] = _bf16(o.T * sg_ref[0].astype(jnp.float32))


def _sb_attn(q, k, vt, sg, tri):
    B, S, dw = q.shape
    nt = S // TILE
    hw = 2 * HEAD_DIM
    return pl.pallas_call(
        _sb_attn_kernel,
        grid=(B, dw // hw, nt),
        in_specs=[
            pl.BlockSpec((1, TILE, hw), lambda b, h, i: (b, i, h)),
            pl.BlockSpec((1, S, hw), lambda b, h, i: (b, 0, h)),
            pl.BlockSpec((1, nt, hw, TILE), lambda b, h, i: (b, 0, h, 0)),
            pl.BlockSpec((1, TILE, hw), lambda b, h, i: (b, i, h)),
            pl.BlockSpec((TILE, TILE), lambda b, h, i: (0, 0)),
        ],
        out_specs=pl.BlockSpec((1, TILE, hw), lambda b, h, i: (b, i, h)),
        out_shape=jax.ShapeDtypeStruct((B, S, dw), jnp.bfloat16),
        scratch_shapes=[pltpu.VMEM((hw, 2 * TILE), jnp.float32)],
        compiler_params=pltpu.CompilerParams(
            dimension_semantics=("arbitrary", "arbitrary", "arbitrary"),
            vmem_limit_bytes=VMEM_LIMIT),
        name="sb_attn",
    )(q, k, vt, sg, tri)


def _out_b_kernel(yo_ref, qm_ref, sgm_ref, x_ref, km_ref, vmh_ref, wo_ref, out_ref):
    dw = yo_ref.shape[2]
    om = _mem_attention(qm_ref[0], km_ref[0, 0], vmh_ref.at[0, 0])
    ym = _bf16(om * sgm_ref[0].astype(jnp.float32))
    out_ref[0] = x_ref[0] + _dot(yo_ref[0], wo_ref[:dw, :]) + _dot(ym, wo_ref[dw:, :])


def _out_b(yo, qm, sg, x, km, vmh, wo):
    B, S, D = x.shape
    dw = yo.shape[2]
    mw = qm.shape[2]
    M = km.shape[2]
    row = lambda w: pl.BlockSpec((1, ROWS, w), lambda b, i: (b, i, 0))
    return pl.pallas_call(
        _out_b_kernel,
        grid=(B, S // ROWS),
        in_specs=[row(dw), row(mw),
                  pl.BlockSpec((1, ROWS, mw), lambda b, i: (b, i, dw // mw)),
                  row(D),
                  pl.BlockSpec((1, 1, M, mw), lambda b, i: (1, b, 0, 0)),
                  pl.BlockSpec((1, 1, MEM_HEADS, M, mw), lambda b, i: (1, b, 0, 0, 0)),
                  pl.BlockSpec(wo.shape, lambda b, i: (0, 0))],
        out_specs=row(D),
        out_shape=jax.ShapeDtypeStruct((B, S, D), jnp.float32),
        compiler_params=pltpu.CompilerParams(
            dimension_semantics=("arbitrary", "arbitrary"), vmem_limit_bytes=VMEM_LIMIT),
        name="out_b",
    )(yo, qm, sg, x, km, vmh, wo)


def kernel(x, mem, positions, a_norm, a_w_in, a_q_norm, a_k_norm, a_lambda_q1, a_lambda_k1,
           a_lambda_q2, a_lambda_k2, a_subln, a_w_out, kv_norm, w_kv_shared, b_norm, b_w_in,
           b_w_out, mem_norm, mem_w_kv, mem_q_norm, mem_k_norm):
    B, S, D = x.shape
    dw = DIFF_HEADS * 2 * HEAD_DIM
    mw = MEM_HEADS * HEAD_DIM
    f32 = jnp.float32

    grp = jnp.arange(MXU_DIM) // HEAD_DIM
    gmat = _bf16(grp[:, None] == grp[None, :])
    tri = _bf16(jnp.triu(jnp.ones((TILE, TILE), f32)))
    d = jnp.arange(LANES) % HEAD_DIM
    inv_freq = ROPE_THETA ** (-jnp.arange(0, ROPE_DIM, 2, dtype=f32) / ROPE_DIM)
    inv = jnp.where(d < ROPE_DIM, inv_freq[d % (ROPE_DIM // 2)], 0.0).reshape(1, LANES)
    slo = jnp.where(d < ROPE_DIM // 2, -1.0, 0.0).astype(f32).reshape(1, LANES)
    shi = jnp.where((d >= ROPE_DIM // 2) & (d < ROPE_DIM), 1.0, 0.0).astype(f32).reshape(1, LANES)

    wa = a_w_in[0]
    regroup = lambda w: w.reshape(D, 2, DIFF_HEADS, HEAD_DIM).transpose(0, 2, 1, 3).reshape(D, dw)
    wq = _bf16(regroup(wa[:, :dw]))
    wk = _bf16(regroup(wa[:, dw:2 * dw]))
    wvt = _bf16(wa[:, 2 * dw:3 * dw].T)
    wqm = _bf16(wa[:, 3 * dw:3 * dw + mw])
    wg = _bf16(wa[:, 3 * dw + mw:])
    gq = (jnp.tile(a_q_norm[0], 2 * DIFF_HEADS) * (SCALE * LOG2E)).reshape(1, dw)
    gk = jnp.tile(a_k_norm[0], 2 * DIFF_HEADS).reshape(1, dw)
    gqm0 = (jnp.tile(mem_q_norm[0], MEM_HEADS) * SCALE).reshape(1, mw)
    gqm1 = (jnp.tile(mem_q_norm[1], MEM_HEADS) * SCALE).reshape(1, mw)
    lambda_init = 0.8 - 0.6 * math.exp(-0.3 * 0)
    gsub = (a_subln[0] * (1.0 - lambda_init)).reshape(2 * HEAD_DIM, 1)

    km, vmh, lam = _mem_prep(mem, mem_norm, mem_w_kv, mem_k_norm, gmat,
                             a_lambda_q1, a_lambda_k1, a_lambda_q2, a_lambda_k2)

    pos = positions.astype(f32).reshape(B, S, 1)
    q0, k0, vt0, qm0, sg0 = _in_proj_a(x, pos, a_norm[0].reshape(1, D), wq, wk, wvt, wqm, wg,
                                       gq, gk, gqm0, inv, slo, shi, gmat)
    y0 = _diff_attn(lam, q0, k0, vt0, sg0, gsub)

    wb = b_w_in[0]
    x1, k1, vt1, q1, qm1, sg1 = _mid(
        y0, qm0, sg0, x, km, vmh, _bf16(a_w_out[0]), kv_norm.reshape(1, D),
        _bf16(w_kv_shared[:, :dw]), _bf16(w_kv_shared[:, dw:].T), b_norm[0].reshape(1, D),
        _bf16(wb[:, :dw] * (SCALE * LOG2E)), _bf16(wb[:, dw:dw + mw]), _bf16(wb[:, dw + mw:]),
        gqm1, gmat)
    y1 = _sb_attn(q1, k1, vt1, sg1, tri)
    return _out_b(y1, qm1, sg1, x1, km, vmh, _bf16(b_w_out[0]))
```

```python
import functools
import math

import jax
import jax.numpy as jnp
from jax import lax
from jax.experimental import pallas as pl
from jax.experimental.pallas import tpu as pltpu

HEAD_DIM = 64
DIFF_HEADS = 6
SB_HEADS = 12
MEM_HEADS = 4
ROPE_DIM = 16
ROPE_THETA = 500000.0
EPS = 1e-6
SCALE = HEAD_DIM ** -0.5
LOG2E = math.log2(math.e)

LANES = 128
MXU_DIM = 256
VMEM_LIMIT = 48 * 1024 * 1024

TILE = MXU_DIM
ROWS = 2 * TILE
KEY_TILES_PER_STEP = 4
SB_KEY_TILES_PER_STEP = 2
SB_CUTOFF_LOG2 = 152.0


def _key_chunk(seq):
    return min(KEY_TILES_PER_STEP * TILE, seq)


def _bf16(x):
    return x.astype(jnp.bfloat16)


def _dot(a, b):
    return jnp.dot(a, b, preferred_element_type=jnp.float32)


def _dot_nt(a, b):
    return lax.dot_general(a, b, (((1,), (1,)), ((), ())), preferred_element_type=jnp.float32)


def _rms_scale(x):
    return x * lax.rsqrt(jnp.mean(x * x, axis=-1, keepdims=True) + EPS)


def _group_norm(p, gmat, gain):
    sq = p * p
    hi = _bf16(sq)
    lo = _bf16(sq - hi.astype(jnp.float32))
    ss = _dot(hi, gmat) + _dot(lo, gmat)
    return p * lax.rsqrt(ss * (1.0 / HEAD_DIM) + EPS) * gain


def _silu(g):
    return g / (1.0 + jnp.exp(-g))


def _mem_attention(qm, km, vmh_ref):
    head_of_lane = lax.broadcasted_iota(jnp.int32, qm.shape, 1) // HEAD_DIM
    om = None
    for h in range(MEM_HEADS):
        qh = jnp.where(head_of_lane == h, qm, jnp.zeros_like(qm))
        sc = _dot_nt(qh, km)
        m = jnp.max(sc, axis=1, keepdims=True)
        p = jnp.exp(sc - m)
        p = p * (1.0 / jnp.sum(p, axis=1, keepdims=True))
        o = _dot(_bf16(p), vmh_ref[h])
        om = o if om is None else om + o
    return om


def _mem_prep_kernel(mem_ref, nrm_ref, w_ref, gk_ref, gmat_ref, lq1_ref, lk1_ref, lq2_ref,
                     lk2_ref, km_ref, vmh_ref, lam_ref):
    h = _bf16(_rms_scale(mem_ref[0]) * nrm_ref[0])
    kv = _dot(h, w_ref[0])
    mw = MEM_HEADS * HEAD_DIM
    km_ref[0, 0] = _bf16(_group_norm(kv[:, :mw], gmat_ref[...], gk_ref[0]))
    vm = _bf16(kv[:, mw:])
    head_of_lane = lax.broadcasted_iota(jnp.int32, vm.shape, 1) // HEAD_DIM
    for hh in range(MEM_HEADS):
        vmh_ref[0, 0, hh] = jnp.where(head_of_lane == hh, vm, jnp.zeros_like(vm))
    lambda_init = 0.8 - 0.6 * math.exp(-0.3 * 0)
    s1 = jnp.sum(lq1_ref[...] * lk1_ref[...], axis=1, keepdims=True)
    s2 = jnp.sum(lq2_ref[...] * lk2_ref[...], axis=1, keepdims=True)
    lam = jnp.exp(s1) - jnp.exp(s2) + lambda_init
    lam_ref[...] = jnp.broadcast_to(lam, lam_ref.shape)


def _mem_prep(mem, mem_norm, mem_w_kv, mem_k_norm, gmat, lq1, lk1, lq2, lk2):
    B, M, D = mem.shape
    L = mem_norm.shape[0]
    mw = MEM_HEADS * HEAD_DIM
    gk = jnp.tile(mem_k_norm, (1, MEM_HEADS)).reshape(L, 1, mw)
    vec = lambda: pl.BlockSpec((1, HEAD_DIM), lambda l, b: (0, 0))
    return pl.pallas_call(
        _mem_prep_kernel,
        grid=(L, B),
        in_specs=[
            pl.BlockSpec((1, M, D), lambda l, b: (b, 0, 0)),
            pl.BlockSpec((1, 1, D), lambda l, b: (l, 0, 0)),
            pl.BlockSpec((1, D, 2 * mw), lambda l, b: (l, 0, 0)),
            pl.BlockSpec((1, 1, mw), lambda l, b: (l, 0, 0)),
            pl.BlockSpec((mw, mw), lambda l, b: (0, 0)),
            vec(), vec(), vec(), vec(),
        ],
        out_specs=[
            pl.BlockSpec((1, 1, M, mw), lambda l, b: (l, b, 0, 0)),
            pl.BlockSpec((1, 1, MEM_HEADS, M, mw), lambda l, b: (l, b, 0, 0, 0)),
            pl.BlockSpec((8, TILE), lambda l, b: (0, 0)),
        ],
        out_shape=[
            jax.ShapeDtypeStruct((L, B, M, mw), jnp.bfloat16),
            jax.ShapeDtypeStruct((L, B, MEM_HEADS, M, mw), jnp.bfloat16),
            jax.ShapeDtypeStruct((8, TILE), jnp.float32),
        ],
        compiler_params=pltpu.CompilerParams(
            dimension_semantics=("arbitrary", "arbitrary"), vmem_limit_bytes=VMEM_LIMIT),
        name="mem_prep",
    )(mem, mem_norm.reshape(L, 1, D), _bf16(mem_w_kv), gk, gmat, lq1, lk1, lq2, lk2)


def _in_proj_a_kernel(x_ref, pos_ref, nrm_ref, wq_ref, wk_ref, wvt_ref, wqm_ref, wg_ref,
                      gq_ref, gk_ref, gqm_ref, inv_ref, slo_ref, shi_ref, gmat_ref,
                      q_out, k_out, vt_out, qm_out, sg_out):
    h = _bf16(_rms_scale(x_ref[0]) * nrm_ref[...])
    gmat = gmat_ref[...]

    ang = pos_ref[0] * inv_ref[...]
    cos = jnp.cos(ang)
    sin = jnp.sin(ang)
    sin_lo = sin * slo_ref[...]
    sin_hi = sin * shi_ref[...]

    def norm_rope(w_ref, g_ref, out_ref):
        p = _dot(h, w_ref[...])
        for c in range(p.shape[1] // MXU_DIM):
            cs = slice(c * MXU_DIM, (c + 1) * MXU_DIM)
            y = _group_norm(p[:, cs], gmat, g_ref[:, cs])
            for half in range(MXU_DIM // LANES):
                yb = y[:, half * LANES:(half + 1) * LANES]
                y_next = pltpu.roll(yb, LANES - ROPE_DIM // 2, axis=1)
                y_prev = pltpu.roll(yb, ROPE_DIM // 2, axis=1)
                lo = c * MXU_DIM + half * LANES
                out_ref[0, :, lo:lo + LANES] = _bf16(yb * cos + y_next * sin_lo + y_prev * sin_hi)

    norm_rope(wq_ref, gq_ref, q_out)
    norm_rope(wk_ref, gk_ref, k_out)
    vt_out[0, 0] = _bf16(_dot_nt(wvt_ref[...], h))
    qm_out[0] = _bf16(_group_norm(_dot(h, wqm_ref[...]), gmat, gqm_ref[...]))
    sg_out[0] = _bf16(_silu(_dot(h, wg_ref[...])))


def _in_proj_a(x, pos, nrm, wq, wk, wvt, wqm, wg, gq, gk, gqm, inv, slo, shi, gmat):
    B, S, D = x.shape
    rpc = _key_chunk(S) // ROWS
    dw = wq.shape[1]
    mw = wqm.shape[1]
    gw = wg.shape[1]
    full = lambda a: pl.BlockSpec(a.shape, lambda b, i: (0,) * a.ndim)
    row = lambda w: pl.BlockSpec((1, ROWS, w), lambda b, i: (b, i, 0))
    consts = (nrm, wq, wk, wvt, wqm, wg, gq, gk, gqm, inv, slo, shi, gmat)
    return pl.pallas_call(
        _in_proj_a_kernel,
        grid=(B, S // ROWS),
        in_specs=[row(D), row(1)] + [full(a) for a in consts],
        out_specs=[row(dw), row(dw),
                   pl.BlockSpec((1, 1, dw, ROWS), lambda b, i: (b, i // rpc, 0, i % rpc)),
                   row(mw), row(gw)],
        out_shape=[
            jax.ShapeDtypeStruct((B, S, dw), jnp.bfloat16),
            jax.ShapeDtypeStruct((B, S, dw), jnp.bfloat16),
            jax.ShapeDtypeStruct((B, S // (rpc * ROWS), dw, rpc * ROWS), jnp.bfloat16),
            jax.ShapeDtypeStruct((B, S, mw), jnp.bfloat16),
            jax.ShapeDtypeStruct((B, S, gw), jnp.bfloat16),
        ],
        compiler_params=pltpu.CompilerParams(
            dimension_semantics=("arbitrary", "arbitrary"), vmem_limit_bytes=VMEM_LIMIT),
        name="in_proj_a",
    )(x, pos, *consts)


def _diff_attn_kernel(lam_ref, q_ref, k_ref, vt_ref, sg_ref, g_ref, o_ref,
                      acc_ref, s0_ref, s1_ref, ml_ref, *, kchunk):
    qi = pl.program_id(2)
    tiles_per_chunk = kchunk // TILE
    dc = qi // tiles_per_chunk
    q = q_ref[0]
    lane = lax.broadcasted_iota(jnp.int32, q.shape, 1)
    zero = jnp.zeros_like(q)
    qboth = jnp.concatenate(
        [jnp.where(lane < HEAD_DIM, q, zero), jnp.where(lane >= HEAD_DIM, q, zero)], axis=0)

    def scores_into(kc, dst_ref):
        kblk = k_ref[0, pl.ds(pl.multiple_of(kc * kchunk, kchunk), kchunk), :]
        dst_ref[...] = _dot_nt(kblk, qboth)

    diag_tile = qi - dc * tiles_per_chunk
    for ntiles in range(1, tiles_per_chunk + 1):
        @pl.when(diag_tile == ntiles - 1)
        def _():
            scores_into(0, s1_ref)
            nkeys = ntiles * TILE
            kblk = k_ref[0, pl.ds(pl.multiple_of(dc * kchunk, kchunk), nkeys), :]
            s = _dot_nt(kblk, qboth)
            row = lax.broadcasted_iota(jnp.int32, s.shape, 0)
            col = lax.broadcasted_iota(jnp.int32, s.shape, 1) % TILE
            s = jnp.where(row - col <= (ntiles - 1) * TILE, s, -jnp.inf)
            m = jnp.max(s, axis=0, keepdims=True)
            p = jnp.exp2(s - m)
            acc_ref[...] = _dot(vt_ref[0, dc, :, :nkeys], _bf16(p))
            ml_ref[0:1, :] = m
            ml_ref[1:2, :] = jnp.sum(p, axis=0, keepdims=True)

    def step(kc, cur_ref, next_ref):
        scores_into(jnp.minimum(kc + 1, dc - 1), next_ref)
        s = cur_ref[...]
        m_prev = ml_ref[0:1, :]
        m_new = jnp.maximum(m_prev, jnp.max(s, axis=0, keepdims=True))
        p = jnp.exp2(s - m_new)
        alpha = jnp.exp2(m_prev - m_new)
        acc_ref[...] = alpha * acc_ref[...] + _dot(vt_ref[0, kc], _bf16(p))
        ml_ref[0:1, :] = m_new
        ml_ref[1:2, :] = alpha * ml_ref[1:2, :] + jnp.sum(p, axis=0, keepdims=True)

    def body(pair, carry):
        step(2 * pair, s1_ref, s0_ref)
        step(2 * pair + 1, s0_ref, s1_ref)
        return carry

    lax.fori_loop(0, dc // 2, body, 0)

    @pl.when(dc % 2 == 1)
    def _():
        step(dc - 1, s1_ref, s0_ref)

    lam = lam_ref[0:1, :]
    r = 1.0 / ml_ref[1:2, :]
    o = acc_ref[:, :TILE] * r[:, :TILE] - acc_ref[:, TILE:] * (lam * r[:, TILE:])
    o = o * lax.rsqrt(jnp.mean(o * o, axis=0, keepdims=True) + EPS) * g_ref[...]
    o_ref[0] = _bf16(o.T * sg_ref[0].astype(jnp.float32))


def _diff_attn(lam, q, k, vt, sg, gcol):
    B, S, dw = q.shape
    nt = S // TILE
    hw = 2 * HEAD_DIM
    nc, kchunk = vt.shape[1], vt.shape[3]
    return pl.pallas_call(
        functools.partial(_diff_attn_kernel, kchunk=kchunk),
        grid=(B, dw // hw, nt),
        in_specs=[
            pl.BlockSpec((8, TILE), lambda b, h, i: (0, 0)),
            pl.BlockSpec((1, TILE, hw), lambda b, h, i: (b, i, h)),
            pl.BlockSpec((1, S, hw), lambda b, h, i: (b, 0, h)),
            pl.BlockSpec((1, nc, hw, kchunk), lambda b, h, i: (b, 0, h, 0)),
            pl.BlockSpec((1, TILE, hw), lambda b, h, i: (b, i, h)),
            pl.BlockSpec((hw, 1), lambda b, h, i: (0, 0)),
        ],
        out_specs=pl.BlockSpec((1, TILE, hw), lambda b, h, i: (b, i, h)),
        out_shape=jax.ShapeDtypeStruct((B, S, dw), jnp.bfloat16),
        scratch_shapes=[pltpu.VMEM((hw, 2 * TILE), jnp.float32),
                        pltpu.VMEM((kchunk, 2 * TILE), jnp.float32),
                        pltpu.VMEM((kchunk, 2 * TILE), jnp.float32),
                        pltpu.VMEM((8, 2 * TILE), jnp.float32)],
        compiler_params=pltpu.CompilerParams(
            dimension_semantics=("arbitrary", "arbitrary", "arbitrary"),
            vmem_limit_bytes=VMEM_LIMIT),
        name="diff_attn",
    )(lam, q, k, vt, sg, gcol)


def _mid_kernel(yo_ref, qm_ref, sgm_ref, x_ref, km_ref, vmh_ref, wo_ref, gkv_ref, wk_ref,
                wvt_ref, gb_ref, wq_ref, wqm_ref, wg_ref, gqm_ref, gmat_ref,
                x1_out, k_out, vt_out, q_out, qm_out, sg_out):
    dw = yo_ref.shape[2]
    om = _mem_attention(qm_ref[0], km_ref[0, 0], vmh_ref.at[0, 0])
    ym = _bf16(om * sgm_ref[0].astype(jnp.float32))
    x1 = x_ref[0] + _dot(yo_ref[0], wo_ref[:dw, :]) + _dot(ym, wo_ref[dw:, :])
    x1_out[0] = x1
    xn = _rms_scale(x1)
    hk = _bf16(xn * gkv_ref[...])
    k_out[0] = _bf16(_dot(hk, wk_ref[...]))
    vt = _bf16(_dot_nt(wvt_ref[...], hk))
    for j in range(vt_out.shape[1]):
        vt_out[0, j] = vt[:, j * TILE:(j + 1) * TILE]
    hb = _bf16(xn * gb_ref[...])
    q_out[0] = _bf16(_dot(hb, wq_ref[...]))
    qm_out[0] = _bf16(_group_norm(_dot(hb, wqm_ref[...]), gmat_ref[...], gqm_ref[...]))
    sg_out[0] = _bf16(_silu(_dot(hb, wg_ref[...])))


def _mid(yo, qm, sg, x, km, vmh, wo, gkv, wk, wvt, gb, wq, wqm, wg, gqm, gmat):
    B, S, D = x.shape
    dw = yo.shape[2]
    mw = qm.shape[2]
    gw = wg.shape[1]
    M = km.shape[2]
    full = lambda a: pl.BlockSpec(a.shape, lambda b, i: (0,) * a.ndim)
    row = lambda w: pl.BlockSpec((1, ROWS, w), lambda b, i: (b, i, 0))
    consts = (wo, gkv, wk, wvt, gb, wq, wqm, wg, gqm, gmat)
    return pl.pallas_call(
        _mid_kernel,
        grid=(B, S // ROWS),
        in_specs=[row(dw), row(mw),
                  pl.BlockSpec((1, ROWS, mw), lambda b, i: (b, i, dw // mw)),
                  row(D),
                  pl.BlockSpec((1, 1, M, mw), lambda b, i: (0, b, 0, 0)),
                  pl.BlockSpec((1, 1, MEM_HEADS, M, mw), lambda b, i: (0, b, 0, 0, 0)),
                  ] + [full(a) for a in consts],
        out_specs=[row(D), row(dw),
                   pl.BlockSpec((1, ROWS // TILE, dw, TILE), lambda b, i: (b, i, 0, 0)),
                   row(dw), row(mw), row(gw)],
        out_shape=[
            jax.ShapeDtypeStruct((B, S, D), jnp.float32),
            jax.ShapeDtypeStruct((B, S, dw), jnp.bfloat16),
            jax.ShapeDtypeStruct((B, S // TILE, dw, TILE), jnp.bfloat16),
            jax.ShapeDtypeStruct((B, S, dw), jnp.bfloat16),
            jax.ShapeDtypeStruct((B, S, mw), jnp.bfloat16),
            jax.ShapeDtypeStruct((B, S, gw), jnp.bfloat16),
        ],
        compiler_params=pltpu.CompilerParams(
            dimension_semantics=("arbitrary", "arbitrary"), vmem_limit_bytes=VMEM_LIMIT),
        name="mid",
    )(yo, qm, sg, x, km, vmh, *consts)


def _sb_attn_kernel(q_ref, k_ref, vt_ref, sg_ref, tri_ref, o_ref, acc_ref):
    qi = pl.program_id(2)
    q = q_ref[0]
    lane = lax.broadcasted_iota(jnp.int32, q.shape, 1)
    zero = jnp.zeros_like(q)
    qboth = jnp.concatenate(
        [jnp.where(lane < HEAD_DIM, q, zero), jnp.where(lane >= HEAD_DIM, q, zero)], axis=0)
    tri = tri_ref[...]

    def window(t0, ntiles, right, valid, first):
        kblk = k_ref[0, pl.ds(pl.multiple_of(t0 * TILE, TILE), ntiles * TILE), :]
        z = _dot_nt(kblk, qboth)
        sp = jnp.maximum(z, 0.0) + jnp.log2(1.0 + jnp.exp2(-jnp.abs(z)))
        if valid is not None:
            sp = jnp.where(valid, sp, 0.0)
        hi = _bf16(sp)
        lo = _bf16(sp - hi.astype(jnp.float32))
        contrib = None
        for j in reversed(range(ntiles)):
            ks = slice(j * TILE, (j + 1) * TILE)
            csum = _dot(tri, hi[ks]) + _dot(tri, lo[ks])
            a = jnp.exp2(z[ks] - csum - right)
            if valid is not None:
                a = jnp.where(valid[ks], a, 0.0)
            o = _dot(vt_ref[0, t0 + j], _bf16(a))
            contrib = o if contrib is None else contrib + o
            right = right + csum[0:1, :]
        if first:
            acc_ref[...] = contrib
        else:
            acc_ref[...] += contrib
        return right

    def unfinished(right):
        smallest = jnp.min(right, axis=1, keepdims=True)
        return (smallest[0, 0] <= SB_CUTOFF_LOG2).astype(jnp.int32)

    nwin = SB_KEY_TILES_PER_STEP
    t0 = jnp.maximum(qi - (nwin - 1), 0)
    row = lax.broadcasted_iota(jnp.int32, (nwin * TILE, 2 * TILE), 0)
    col = lax.broadcasted_iota(jnp.int32, (nwin * TILE, 2 * TILE), 1) % TILE
    valid = row - col < (qi - t0) * TILE
    right = window(t0, nwin, jnp.zeros((1, 2 * TILE), jnp.float32), valid, True)

    def cond(state):
        return jnp.logical_and(state[0] >= nwin, state[1] > 0)

    def body(state):
        t = state[0] - nwin
        new = window(t, nwin, state[2], None, False)
        return (t, unfinished(new), new)

    t, more, right = lax.while_loop(cond, body, (t0, unfinished(right), right))

    for rem in range(1, nwin):
        @pl.when(jnp.logical_and(t == rem, more > 0))
        def _():
            window(0, rem, right, None, False)

    acc = acc_ref[...]
    o = jnp.concatenate([acc[:HEAD_DIM, :TILE], acc[HEAD_DIM:, TILE:]], axis=0)
    o_ref[0] = _bf16(o.T * sg_ref[0].astype(jnp.float32))


def _sb_attn(q, k, vt, sg, tri):
    B, S, dw = q.shape
    nt = S // TILE
    hw = 2 * HEAD_DIM
    return pl.pallas_call(
        _sb_attn_kernel,
        grid=(B, dw // hw, nt),
        in_specs=[
            pl.BlockSpec((1, TILE, hw), lambda b, h, i: (b, i, h)),
            pl.BlockSpec((1, S, hw), lambda b, h, i: (b, 0, h)),
            pl.BlockSpec((1, nt, hw, TILE), lambda b, h, i: (b, 0, h, 0)),
            pl.BlockSpec((1, TILE, hw), lambda b, h, i: (b, i, h)),
            pl.BlockSpec((TILE, TILE), lambda b, h, i: (0, 0)),
        ],
        out_specs=pl.BlockSpec((1, TILE, hw), lambda b, h, i: (b, i, h)),
        out_shape=jax.ShapeDtypeStruct((B, S, dw), jnp.bfloat16),
        scratch_shapes=[pltpu.VMEM((hw, 2 * TILE), jnp.float32)],
        compiler_params=pltpu.CompilerParams(
            dimension_semantics=("arbitrary", "arbitrary", "arbitrary"),
            vmem_limit_bytes=VMEM_LIMIT),
        name="sb_attn",
    )(q, k, vt, sg, tri)


def _out_b_kernel(yo_ref, qm_ref, sgm_ref, x_ref, km_ref, vmh_ref, wo_ref, out_ref):
    dw = yo_ref.shape[2]
    om = _mem_attention(qm_ref[0], km_ref[0, 0], vmh_ref.at[0, 0])
    ym = _bf16(om * sgm_ref[0].astype(jnp.float32))
    out_ref[0] = x_ref[0] + _dot(yo_ref[0], wo_ref[:dw, :]) + _dot(ym, wo_ref[dw:, :])


def _out_b(yo, qm, sg, x, km, vmh, wo):
    B, S, D = x.shape
    dw = yo.shape[2]
    mw = qm.shape[2]
    M = km.shape[2]
    row = lambda w: pl.BlockSpec((1, ROWS, w), lambda b, i: (b, i, 0))
    return pl.pallas_call(
        _out_b_kernel,
        grid=(B, S // ROWS),
        in_specs=[row(dw), row(mw),
                  pl.BlockSpec((1, ROWS, mw), lambda b, i: (b, i, dw // mw)),
                  row(D),
                  pl.BlockSpec((1, 1, M, mw), lambda b, i: (1, b, 0, 0)),
                  pl.BlockSpec((1, 1, MEM_HEADS, M, mw), lambda b, i: (1, b, 0, 0, 0)),
                  pl.BlockSpec(wo.shape, lambda b, i: (0, 0))],
        out_specs=row(D),
        out_shape=jax.ShapeDtypeStruct((B, S, D), jnp.float32),
        compiler_params=pltpu.CompilerParams(
            dimension_semantics=("arbitrary", "arbitrary"), vmem_limit_bytes=VMEM_LIMIT),
        name="out_b",
    )(yo, qm, sg, x, km, vmh, wo)


def kernel(x, mem, positions, a_norm, a_w_in, a_q_norm, a_k_norm, a_lambda_q1, a_lambda_k1,
           a_lambda_q2, a_lambda_k2, a_subln, a_w_out, kv_norm, w_kv_shared, b_norm, b_w_in,
           b_w_out, mem_norm, mem_w_kv, mem_q_norm, mem_k_norm):
    B, S, D = x.shape
    dw = DIFF_HEADS * 2 * HEAD_DIM
    mw = MEM_HEADS * HEAD_DIM
    f32 = jnp.float32

    grp = jnp.arange(MXU_DIM) // HEAD_DIM
    gmat = _bf16(grp[:, None] == grp[None, :])
    tri = _bf16(jnp.triu(jnp.ones((TILE, TILE), f32)))
    d = jnp.arange(LANES) % HEAD_DIM
    inv_freq = ROPE_THETA ** (-jnp.arange(0, ROPE_DIM, 2, dtype=f32) / ROPE_DIM)
    inv = jnp.where(d < ROPE_DIM, inv_freq[d % (ROPE_DIM // 2)], 0.0).reshape(1, LANES)
    slo = jnp.where(d < ROPE_DIM // 2, -1.0, 0.0).astype(f32).reshape(1, LANES)
    shi = jnp.where((d >= ROPE_DIM // 2) & (d < ROPE_DIM), 1.0, 0.0).astype(f32).reshape(1, LANES)

    wa = a_w_in[0]
    regroup = lambda w: w.reshape(D, 2, DIFF_HEADS, HEAD_DIM).transpose(0, 2, 1, 3).reshape(D, dw)
    wq = _bf16(regroup(wa[:, :dw]))
    wk = _bf16(regroup(wa[:, dw:2 * dw]))
    wvt = _bf16(wa[:, 2 * dw:3 * dw].T)
    wqm = _bf16(wa[:, 3 * dw:3 * dw + mw])
    wg = _bf16(wa[:, 3 * dw + mw:])
    gq = (jnp.tile(a_q_norm[0], 2 * DIFF_HEADS) * (SCALE * LOG2E)).reshape(1, dw)
    gk = jnp.tile(a_k_norm[0], 2 * DIFF_HEADS).reshape(1, dw)
    gqm0 = (jnp.tile(mem_q_norm[0], MEM_HEADS) * SCALE).reshape(1, mw)
    gqm1 = (jnp.tile(mem_q_norm[1], MEM_HEADS) * SCALE).reshape(1, mw)
    lambda_init = 0.8 - 0.6 * math.exp(-0.3 * 0)
    gsub = (a_subln[0] * (1.0 - lambda_init)).reshape(2 * HEAD_DIM, 1)

    km, vmh, lam = _mem_prep(mem, mem_norm, mem_w_kv, mem_k_norm, gmat,
                             a_lambda_q1, a_lambda_k1, a_lambda_q2, a_lambda_k2)

    pos = positions.astype(f32).reshape(B, S, 1)
    q0, k0, vt0, qm0, sg0 = _in_proj_a(x, pos, a_norm[0].reshape(1, D), wq, wk, wvt, wqm, wg,
                                       gq, gk, gqm0, inv, slo, shi, gmat)
    y0 = _diff_attn(lam, q0, k0, vt0, sg0, gsub)

    wb = b_w_in[0]
    x1, k1, vt1, q1, qm1, sg1 = _mid(
        y0, qm0, sg0, x, km, vmh, _bf16(a_w_out[0]), kv_norm.reshape(1, D),
        _bf16(w_kv_shared[:, :dw]), _bf16(w_kv_shared[:, dw:].T), b_norm[0].reshape(1, D),
        _bf16(wb[:, :dw] * (SCALE * LOG2E)), _bf16(wb[:, dw:dw + mw]), _bf16(wb[:, dw + mw:]),
        gqm1, gmat)
    y1 = _sb_attn(q1, k1, vt1, sg1, tri)
    return _out_b(y1, qm1, sg1, x1, km, vmh, _bf16(b_w_out[0]))
```

```python
import functools
import math

import jax
import jax.numpy as jnp
from jax import lax
from jax.experimental import pallas as pl
from jax.experimental.pallas import tpu as pltpu

HEAD_DIM = 64
DIFF_HEADS = 6
SB_HEADS = 12
MEM_HEADS = 4
ROPE_DIM = 16
ROPE_THETA = 500000.0
EPS = 1e-6
SCALE = HEAD_DIM ** -0.5
LOG2E = math.log2(math.e)

LANES = 128
MXU_DIM = 256
VMEM_LIMIT = 48 * 1024 * 1024

TILE = MXU_DIM
ROWS = 2 * TILE
KEY_TILES_PER_STEP = 4
CHUNKS_PER_TRIP = 4
SB_KEY_TILES_PER_STEP = 2
SB_CUTOFF_LOG2 = 152.0


def _key_chunk(seq):
    return min(KEY_TILES_PER_STEP * TILE, seq)


def _bf16(x):
    return x.astype(jnp.bfloat16)


def _dot(a, b):
    return jnp.dot(a, b, preferred_element_type=jnp.float32)


def _dot_nt(a, b):
    return lax.dot_general(a, b, (((1,), (1,)), ((), ())), preferred_element_type=jnp.float32)


def _rms_scale(x):
    return x * lax.rsqrt(jnp.mean(x * x, axis=-1, keepdims=True) + EPS)


def _group_norm(p, gmat, gain):
    sq = p * p
    hi = _bf16(sq)
    lo = _bf16(sq - hi.astype(jnp.float32))
    ss = _dot(hi, gmat) + _dot(lo, gmat)
    return p * lax.rsqrt(ss * (1.0 / HEAD_DIM) + EPS) * gain


def _silu(g):
    return g / (1.0 + jnp.exp(-g))


def _mem_attention(qm, km, vmh_ref):
    head_of_lane = lax.broadcasted_iota(jnp.int32, qm.shape, 1) // HEAD_DIM
    om = None
    for h in range(MEM_HEADS):
        qh = jnp.where(head_of_lane == h, qm, jnp.zeros_like(qm))
        sc = _dot_nt(qh, km)
        m = jnp.max(sc, axis=1, keepdims=True)
        p = jnp.exp(sc - m)
        p = p * (1.0 / jnp.sum(p, axis=1, keepdims=True))
        o = _dot(_bf16(p), vmh_ref[h])
        om = o if om is None else om + o
    return om


def _mem_prep_kernel(mem_ref, nrm_ref, w_ref, gk_ref, gmat_ref, lq1_ref, lk1_ref, lq2_ref,
                     lk2_ref, km_ref, vmh_ref, lam_ref):
    h = _bf16(_rms_scale(mem_ref[0]) * nrm_ref[0])
    kv = _dot(h, w_ref[0])
    mw = MEM_HEADS * HEAD_DIM
    km_ref[0, 0] = _bf16(_group_norm(kv[:, :mw], gmat_ref[...], gk_ref[0]))
    vm = _bf16(kv[:, mw:])
    head_of_lane = lax.broadcasted_iota(jnp.int32, vm.shape, 1) // HEAD_DIM
    for hh in range(MEM_HEADS):
        vmh_ref[0, 0, hh] = jnp.where(head_of_lane == hh, vm, jnp.zeros_like(vm))
    lambda_init = 0.8 - 0.6 * math.exp(-0.3 * 0)
    s1 = jnp.sum(lq1_ref[...] * lk1_ref[...], axis=1, keepdims=True)
    s2 = jnp.sum(lq2_ref[...] * lk2_ref[...], axis=1, keepdims=True)
    lam = jnp.exp(s1) - jnp.exp(s2) + lambda_init
    lam_ref[...] = jnp.broadcast_to(lam, lam_ref.shape)


def _mem_prep(mem, mem_norm, mem_w_kv, mem_k_norm, gmat, lq1, lk1, lq2, lk2):
    B, M, D = mem.shape
    L = mem_norm.shape[0]
    mw = MEM_HEADS * HEAD_DIM
    gk = jnp.tile(mem_k_norm, (1, MEM_HEADS)).reshape(L, 1, mw)
    vec = lambda: pl.BlockSpec((1, HEAD_DIM), lambda l, b: (0, 0))
    return pl.pallas_call(
        _mem_prep_kernel,
        grid=(L, B),
        in_specs=[
            pl.BlockSpec((1, M, D), lambda l, b: (b, 0, 0)),
            pl.BlockSpec((1, 1, D), lambda l, b: (l, 0, 0)),
            pl.BlockSpec((1, D, 2 * mw), lambda l, b: (l, 0, 0)),
            pl.BlockSpec((1, 1, mw), lambda l, b: (l, 0, 0)),
            pl.BlockSpec((mw, mw), lambda l, b: (0, 0)),
            vec(), vec(), vec(), vec(),
        ],
        out_specs=[
            pl.BlockSpec((1, 1, M, mw), lambda l, b: (l, b, 0, 0)),
            pl.BlockSpec((1, 1, MEM_HEADS, M, mw), lambda l, b: (l, b, 0, 0, 0)),
            pl.BlockSpec((8, TILE), lambda l, b: (0, 0)),
        ],
        out_shape=[
            jax.ShapeDtypeStruct((L, B, M, mw), jnp.bfloat16),
            jax.ShapeDtypeStruct((L, B, MEM_HEADS, M, mw), jnp.bfloat16),
            jax.ShapeDtypeStruct((8, TILE), jnp.float32),
        ],
        compiler_params=pltpu.CompilerParams(
            dimension_semantics=("arbitrary", "arbitrary"), vmem_limit_bytes=VMEM_LIMIT),
        name="mem_prep",
    )(mem, mem_norm.reshape(L, 1, D), _bf16(mem_w_kv), gk, gmat, lq1, lk1, lq2, lk2)


def _in_proj_a_kernel(x_ref, pos_ref, nrm_ref, wq_ref, wk_ref, wvt_ref, wqm_ref, wg_ref,
                      gq_ref, gk_ref, gqm_ref, inv_ref, slo_ref, shi_ref, gmat_ref,
                      q_out, k_out, vt_out, qm_out, sg_out):
    h = _bf16(_rms_scale(x_ref[0]) * nrm_ref[...])
    gmat = gmat_ref[...]

    ang = pos_ref[0] * inv_ref[...]
    cos = jnp.cos(ang)
    sin = jnp.sin(ang)
    sin_lo = sin * slo_ref[...]
    sin_hi = sin * shi_ref[...]

    def norm_rope(w_ref, g_ref, out_ref):
        p = _dot(h, w_ref[...])
        for c in range(p.shape[1] // MXU_DIM):
            cs = slice(c * MXU_DIM, (c + 1) * MXU_DIM)
            y = _group_norm(p[:, cs], gmat, g_ref[:, cs])
            for half in range(MXU_DIM // LANES):
                yb = y[:, half * LANES:(half + 1) * LANES]
                y_next = pltpu.roll(yb, LANES - ROPE_DIM // 2, axis=1)
                y_prev = pltpu.roll(yb, ROPE_DIM // 2, axis=1)
                lo = c * MXU_DIM + half * LANES
                out_ref[0, :, lo:lo + LANES] = _bf16(yb * cos + y_next * sin_lo + y_prev * sin_hi)

    norm_rope(wq_ref, gq_ref, q_out)
    norm_rope(wk_ref, gk_ref, k_out)
    vt_out[0, 0] = _bf16(_dot_nt(wvt_ref[...], h))
    qm_out[0] = _bf16(_group_norm(_dot(h, wqm_ref[...]), gmat, gqm_ref[...]))
    sg_out[0] = _bf16(_silu(_dot(h, wg_ref[...])))


def _in_proj_a(x, pos, nrm, wq, wk, wvt, wqm, wg, gq, gk, gqm, inv, slo, shi, gmat):
    B, S, D = x.shape
    rpc = _key_chunk(S) // ROWS
    dw = wq.shape[1]
    mw = wqm.shape[1]
    gw = wg.shape[1]
    full = lambda a: pl.BlockSpec(a.shape, lambda b, i: (0,) * a.ndim)
    row = lambda w: pl.BlockSpec((1, ROWS, w), lambda b, i: (b, i, 0))
    consts = (nrm, wq, wk, wvt, wqm, wg, gq, gk, gqm, inv, slo, shi, gmat)
    return pl.pallas_call(
        _in_proj_a_kernel,
        grid=(B, S // ROWS),
        in_specs=[row(D), row(1)] + [full(a) for a in consts],
        out_specs=[row(dw), row(dw),
                   pl.BlockSpec((1, 1, dw, ROWS), lambda b, i: (b, i // rpc, 0, i % rpc)),
                   row(mw), row(gw)],
        out_shape=[
            jax.ShapeDtypeStruct((B, S, dw), jnp.bfloat16),
            jax.ShapeDtypeStruct((B, S, dw), jnp.bfloat16),
            jax.ShapeDtypeStruct((B, S // (rpc * ROWS), dw, rpc * ROWS), jnp.bfloat16),
            jax.ShapeDtypeStruct((B, S, mw), jnp.bfloat16),
            jax.ShapeDtypeStruct((B, S, gw), jnp.bfloat16),
        ],
        compiler_params=pltpu.CompilerParams(
            dimension_semantics=("arbitrary", "arbitrary"), vmem_limit_bytes=VMEM_LIMIT),
        name="in_proj_a",
    )(x, pos, *consts)


def _diff_attn_kernel(lam_ref, q_ref, k_ref, vt_ref, sg_ref, g_ref, o_ref,
                      acc_ref, s0_ref, s1_ref, ml_ref, *, kchunk):
    qi = pl.program_id(2)
    tiles_per_chunk = kchunk // TILE
    dc = qi // tiles_per_chunk
    q = q_ref[0]
    lane = lax.broadcasted_iota(jnp.int32, q.shape, 1)
    zero = jnp.zeros_like(q)
    qboth = jnp.concatenate(
        [jnp.where(lane < HEAD_DIM, q, zero), jnp.where(lane >= HEAD_DIM, q, zero)], axis=0)

    def scores_into(kc, dst_ref):
        kblk = k_ref[0, pl.ds(pl.multiple_of(kc * kchunk, kchunk), kchunk), :]
        dst_ref[...] = _dot_nt(kblk, qboth)

    diag_tile = qi - dc * tiles_per_chunk
    for ntiles in range(1, tiles_per_chunk + 1):
        @pl.when(diag_tile == ntiles - 1)
        def _():
            scores_into(0, s1_ref)
            nkeys = ntiles * TILE
            kblk = k_ref[0, pl.ds(pl.multiple_of(dc * kchunk, kchunk), nkeys), :]
            s = _dot_nt(kblk, qboth)
            row = lax.broadcasted_iota(jnp.int32, s.shape, 0)
            col = lax.broadcasted_iota(jnp.int32, s.shape, 1) % TILE
            s = jnp.where(row - col <= (ntiles - 1) * TILE, s, -jnp.inf)
            m = jnp.max(s, axis=0, keepdims=True)
            p = jnp.exp2(s - m)
            acc_ref[...] = _dot(vt_ref[0, dc, :, :nkeys], _bf16(p))
            ml_ref[0:1, :] = m
            ml_ref[1:2, :] = jnp.sum(p, axis=0, keepdims=True)

    def step(kc, cur_ref, next_ref):
        scores_into(jnp.minimum(kc + 1, dc - 1), next_ref)
        s = cur_ref[...]
        m_prev = ml_ref[0:1, :]
        m_new = jnp.maximum(m_prev, jnp.max(s, axis=0, keepdims=True))
        p = jnp.exp2(s - m_new)
        alpha = jnp.exp2(m_prev - m_new)
        acc_ref[...] = alpha * acc_ref[...] + _dot(vt_ref[0, kc], _bf16(p))
        ml_ref[0:1, :] = m_new
        ml_ref[1:2, :] = alpha * ml_ref[1:2, :] + jnp.sum(p, axis=0, keepdims=True)

    def run(first, count):
        for u in range(count):
            if u % 2 == 0:
                step(first + u, s1_ref, s0_ref)
            else:
                step(first + u, s0_ref, s1_ref)

    def body(group, carry):
        run(CHUNKS_PER_TRIP * group, CHUNKS_PER_TRIP)
        return carry

    lax.fori_loop(0, dc // CHUNKS_PER_TRIP, body, 0)
    left = dc % CHUNKS_PER_TRIP

    @pl.when(left >= 2)
    def _():
        run(dc - left, 2)

    @pl.when(left % 2 == 1)
    def _():
        run(dc - 1, 1)

    lam = lam_ref[0:1, :]
    r = 1.0 / ml_ref[1:2, :]
    o = acc_ref[:, :TILE] * r[:, :TILE] - acc_ref[:, TILE:] * (lam * r[:, TILE:])
    o = o * lax.rsqrt(jnp.mean(o * o, axis=0, keepdims=True) + EPS) * g_ref[...]
    o_ref[0] = _bf16(o.T * sg_ref[0].astype(jnp.float32))


def _diff_attn(lam, q, k, vt, sg, gcol):
    B, S, dw = q.shape
    nt = S // TILE
    hw = 2 * HEAD_DIM
    nc, kchunk = vt.shape[1], vt.shape[3]
    return pl.pallas_call(
        functools.partial(_diff_attn_kernel, kchunk=kchunk),
        grid=(B, dw // hw, nt),
        in_specs=[
            pl.BlockSpec((8, TILE), lambda b, h, i: (0, 0)),
            pl.BlockSpec((1, TILE, hw), lambda b, h, i: (b, i, h)),
            pl.BlockSpec((1, S, hw), lambda b, h, i: (b, 0, h)),
            pl.BlockSpec((1, nc, hw, kchunk), lambda b, h, i: (b, 0, h, 0)),
            pl.BlockSpec((1, TILE, hw), lambda b, h, i: (b, i, h)),
            pl.BlockSpec((hw, 1), lambda b, h, i: (0, 0)),
        ],
        out_specs=pl.BlockSpec((1, TILE, hw), lambda b, h, i: (b, i, h)),
        out_shape=jax.ShapeDtypeStruct((B, S, dw), jnp.bfloat16),
        scratch_shapes=[pltpu.VMEM((hw, 2 * TILE), jnp.float32),
                        pltpu.VMEM((kchunk, 2 * TILE), jnp.float32),
                        pltpu.VMEM((kchunk, 2 * TILE), jnp.float32),
                        pltpu.VMEM((8, 2 * TILE), jnp.float32)],
        compiler_params=pltpu.CompilerParams(
            dimension_semantics=("arbitrary", "arbitrary", "arbitrary"),
            vmem_limit_bytes=VMEM_LIMIT),
        name="diff_attn",
    )(lam, q, k, vt, sg, gcol)


def _mid_kernel(yo_ref, qm_ref, sgm_ref, x_ref, km_ref, vmh_ref, wo_ref, gkv_ref, wk_ref,
                wvt_ref, gb_ref, wq_ref, wqm_ref, wg_ref, gqm_ref, gmat_ref,
                x1_out, k_out, vt_out, q_out, qm_out, sg_out):
    dw = yo_ref.shape[2]
    om = _mem_attention(qm_ref[0], km_ref[0, 0], vmh_ref.at[0, 0])
    ym = _bf16(om * sgm_ref[0].astype(jnp.float32))
    x1 = x_ref[0] + _dot(yo_ref[0], wo_ref[:dw, :]) + _dot(ym, wo_ref[dw:, :])
    x1_out[0] = x1
    xn = _rms_scale(x1)
    hk = _bf16(xn * gkv_ref[...])
    k_out[0] = _bf16(_dot(hk, wk_ref[...]))
    vt = _bf16(_dot_nt(wvt_ref[...], hk))
    for j in range(vt_out.shape[1]):
        vt_out[0, j] = vt[:, j * TILE:(j + 1) * TILE]
    hb = _bf16(xn * gb_ref[...])
    q_out[0] = _bf16(_dot(hb, wq_ref[...]))
    qm_out[0] = _bf16(_group_norm(_dot(hb, wqm_ref[...]), gmat_ref[...], gqm_ref[...]))
    sg_out[0] = _bf16(_silu(_dot(hb, wg_ref[...])))


def _mid(yo, qm, sg, x, km, vmh, wo, gkv, wk, wvt, gb, wq, wqm, wg, gqm, gmat):
    B, S, D = x.shape
    dw = yo.shape[2]
    mw = qm.shape[2]
    gw = wg.shape[1]
    M = km.shape[2]
    full = lambda a: pl.BlockSpec(a.shape, lambda b, i: (0,) * a.ndim)
    row = lambda w: pl.BlockSpec((1, ROWS, w), lambda b, i: (b, i, 0))
    consts = (wo, gkv, wk, wvt, gb, wq, wqm, wg, gqm, gmat)
    return pl.pallas_call(
        _mid_kernel,
        grid=(B, S // ROWS),
        in_specs=[row(dw), row(mw),
                  pl.BlockSpec((1, ROWS, mw), lambda b, i: (b, i, dw // mw)),
                  row(D),
                  pl.BlockSpec((1, 1, M, mw), lambda b, i: (0, b, 0, 0)),
                  pl.BlockSpec((1, 1, MEM_HEADS, M, mw), lambda b, i: (0, b, 0, 0, 0)),
                  ] + [full(a) for a in consts],
        out_specs=[row(D), row(dw),
                   pl.BlockSpec((1, ROWS // TILE, dw, TILE), lambda b, i: (b, i, 0, 0)),
                   row(dw), row(mw), row(gw)],
        out_shape=[
            jax.ShapeDtypeStruct((B, S, D), jnp.float32),
            jax.ShapeDtypeStruct((B, S, dw), jnp.bfloat16),
            jax.ShapeDtypeStruct((B, S // TILE, dw, TILE), jnp.bfloat16),
            jax.ShapeDtypeStruct((B, S, dw), jnp.bfloat16),
            jax.ShapeDtypeStruct((B, S, mw), jnp.bfloat16),
            jax.ShapeDtypeStruct((B, S, gw), jnp.bfloat16),
        ],
        compiler_params=pltpu.CompilerParams(
            dimension_semantics=("arbitrary", "arbitrary"), vmem_limit_bytes=VMEM_LIMIT),
        name="mid",
    )(yo, qm, sg, x, km, vmh, *consts)


def _sb_attn_kernel(q_ref, k_ref, vt_ref, sg_ref, tri_ref, o_ref, acc_ref):
    qi = pl.program_id(2)
    q = q_ref[0]
    lane = lax.broadcasted_iota(jnp.int32, q.shape, 1)
    zero = jnp.zeros_like(q)
    qboth = jnp.concatenate(
        [jnp.where(lane < HEAD_DIM, q, zero), jnp.where(lane >= HEAD_DIM, q, zero)], axis=0)
    tri = tri_ref[...]

    def window(t0, ntiles, right, valid, first):
        kblk = k_ref[0, pl.ds(pl.multiple_of(t0 * TILE, TILE), ntiles * TILE), :]
        z = _dot_nt(kblk, qboth)
        sp = jnp.maximum(z, 0.0) + jnp.log2(1.0 + jnp.exp2(-jnp.abs(z)))
        if valid is not None:
            sp = jnp.where(valid, sp, 0.0)
        hi = _bf16(sp)
        lo = _bf16(sp - hi.astype(jnp.float32))
        contrib = None
        for j in reversed(range(ntiles)):
            ks = slice(j * TILE, (j + 1) * TILE)
            csum = _dot(tri, hi[ks]) + _dot(tri, lo[ks])
            a = jnp.exp2(z[ks] - csum - right)
            if valid is not None:
                a = jnp.where(valid[ks], a, 0.0)
            o = _dot(vt_ref[0, t0 + j], _bf16(a))
            contrib = o if contrib is None else contrib + o
            right = right + csum[0:1, :]
        if first:
            acc_ref[...] = contrib
        else:
            acc_ref[...] += contrib
        return right

    def unfinished(right):
        smallest = jnp.min(right, axis=1, keepdims=True)
        return (smallest[0, 0] <= SB_CUTOFF_LOG2).astype(jnp.int32)

    nwin = SB_KEY_TILES_PER_STEP
    t0 = jnp.maximum(qi - (nwin - 1), 0)
    row = lax.broadcasted_iota(jnp.int32, (nwin * TILE, 2 * TILE), 0)
    col = lax.broadcasted_iota(jnp.int32, (nwin * TILE, 2 * TILE), 1) % TILE
    valid = row - col < (qi - t0) * TILE
    right = window(t0, nwin, jnp.zeros((1, 2 * TILE), jnp.float32), valid, True)

    def cond(state):
        return jnp.logical_and(state[0] >= nwin, state[1] > 0)

    def body(state):
        t = state[0] - nwin
        new = window(t, nwin, state[2], None, False)
        return (t, unfinished(new), new)

    t, more, right = lax.while_loop(cond, body, (t0, unfinished(right), right))

    for rem in range(1, nwin):
        @pl.when(jnp.logical_and(t == rem, more > 0))
        def _():
            window(0, rem, right, None, False)

    acc = acc_ref[...]
    o = jnp.concatenate([acc[:HEAD_DIM, :TILE], acc[HEAD_DIM:, TILE:]], axis=0)
    o_ref[0] = _bf16(o.T * sg_ref[0].astype(jnp.float32))


def _sb_attn(q, k, vt, sg, tri):
    B, S, dw = q.shape
    nt = S // TILE
    hw = 2 * HEAD_DIM
    return pl.pallas_call(
        _sb_attn_kernel,
        grid=(B, dw // hw, nt),
        in_specs=[
            pl.BlockSpec((1, TILE, hw), lambda b, h, i: (b, i, h)),
            pl.BlockSpec((1, S, hw), lambda b, h, i: (b, 0, h)),
            pl.BlockSpec((1, nt, hw, TILE), lambda b, h, i: (b, 0, h, 0)),
            pl.BlockSpec((1, TILE, hw), lambda b, h, i: (b, i, h)),
            pl.BlockSpec((TILE, TILE), lambda b, h, i: (0, 0)),
        ],
        out_specs=pl.BlockSpec((1, TILE, hw), lambda b, h, i: (b, i, h)),
        out_shape=jax.ShapeDtypeStruct((B, S, dw), jnp.bfloat16),
        scratch_shapes=[pltpu.VMEM((hw, 2 * TILE), jnp.float32)],
        compiler_params=pltpu.CompilerParams(
            dimension_semantics=("arbitrary", "arbitrary", "arbitrary"),
            vmem_limit_bytes=VMEM_LIMIT),
        name="sb_attn",
    )(q, k, vt, sg, tri)


def _out_b_kernel(yo_ref, qm_ref, sgm_ref, x_ref, km_ref, vmh_ref, wo_ref, out_ref):
    dw = yo_ref.shape[2]
    om = _mem_attention(qm_ref[0], km_ref[0, 0], vmh_ref.at[0, 0])
    ym = _bf16(om * sgm_ref[0].astype(jnp.float32))
    out_ref[0] = x_ref[0] + _dot(yo_ref[0], wo_ref[:dw, :]) + _dot(ym, wo_ref[dw:, :])


def _out_b(yo, qm, sg, x, km, vmh, wo):
    B, S, D = x.shape
    dw = yo.shape[2]
    mw = qm.shape[2]
    M = km.shape[2]
    row = lambda w: pl.BlockSpec((1, ROWS, w), lambda b, i: (b, i, 0))
    return pl.pallas_call(
        _out_b_kernel,
        grid=(B, S // ROWS),
        in_specs=[row(dw), row(mw),
                  pl.BlockSpec((1, ROWS, mw), lambda b, i: (b, i, dw // mw)),
                  row(D),
                  pl.BlockSpec((1, 1, M, mw), lambda b, i: (1, b, 0, 0)),
                  pl.BlockSpec((1, 1, MEM_HEADS, M, mw), lambda b, i: (1, b, 0, 0, 0)),
                  pl.BlockSpec(wo.shape, lambda b, i: (0, 0))],
        out_specs=row(D),
        out_shape=jax.ShapeDtypeStruct((B, S, D), jnp.float32),
        compiler_params=pltpu.CompilerParams(
            dimension_semantics=("arbitrary", "arbitrary"), vmem_limit_bytes=VMEM_LIMIT),
        name="out_b",
    )(yo, qm, sg, x, km, vmh, wo)


def kernel(x, mem, positions, a_norm, a_w_in, a_q_norm, a_k_norm, a_lambda_q1, a_lambda_k1,
           a_lambda_q2, a_lambda_k2, a_subln, a_w_out, kv_norm, w_kv_shared, b_norm, b_w_in,
           b_w_out, mem_norm, mem_w_kv, mem_q_norm, mem_k_norm):
    B, S, D = x.shape
    dw = DIFF_HEADS * 2 * HEAD_DIM
    mw = MEM_HEADS * HEAD_DIM
    f32 = jnp.float32

    grp = jnp.arange(MXU_DIM) // HEAD_DIM
    gmat = _bf16(grp[:, None] == grp[None, :])
    tri = _bf16(jnp.triu(jnp.ones((TILE, TILE), f32)))
    d = jnp.arange(LANES) % HEAD_DIM
    inv_freq = ROPE_THETA ** (-jnp.arange(0, ROPE_DIM, 2, dtype=f32) / ROPE_DIM)
    inv = jnp.where(d < ROPE_DIM, inv_freq[d % (ROPE_DIM // 2)], 0.0).reshape(1, LANES)
    slo = jnp.where(d < ROPE_DIM // 2, -1.0, 0.0).astype(f32).reshape(1, LANES)
    shi = jnp.where((d >= ROPE_DIM // 2) & (d < ROPE_DIM), 1.0, 0.0).astype(f32).reshape(1, LANES)

    wa = a_w_in[0]
    regroup = lambda w: w.reshape(D, 2, DIFF_HEADS, HEAD_DIM).transpose(0, 2, 1, 3).reshape(D, dw)
    wq = _bf16(regroup(wa[:, :dw]))
    wk = _bf16(regroup(wa[:, dw:2 * dw]))
    wvt = _bf16(wa[:, 2 * dw:3 * dw].T)
    wqm = _bf16(wa[:, 3 * dw:3 * dw + mw])
    wg = _bf16(wa[:, 3 * dw + mw:])
    gq = (jnp.tile(a_q_norm[0], 2 * DIFF_HEADS) * (SCALE * LOG2E)).reshape(1, dw)
    gk = jnp.tile(a_k_norm[0], 2 * DIFF_HEADS).reshape(1, dw)
    gqm0 = (jnp.tile(mem_q_norm[0], MEM_HEADS) * SCALE).reshape(1, mw)
    gqm1 = (jnp.tile(mem_q_norm[1], MEM_HEADS) * SCALE).reshape(1, mw)
    lambda_init = 0.8 - 0.6 * math.exp(-0.3 * 0)
    gsub = (a_subln[0] * (1.0 - lambda_init)).reshape(2 * HEAD_DIM, 1)

    km, vmh, lam = _mem_prep(mem, mem_norm, mem_w_kv, mem_k_norm, gmat,
                             a_lambda_q1, a_lambda_k1, a_lambda_q2, a_lambda_k2)

    pos = positions.astype(f32).reshape(B, S, 1)
    q0, k0, vt0, qm0, sg0 = _in_proj_a(x, pos, a_norm[0].reshape(1, D), wq, wk, wvt, wqm, wg,
                                       gq, gk, gqm0, inv, slo, shi, gmat)
    y0 = _diff_attn(lam, q0, k0, vt0, sg0, gsub)

    wb = b_w_in[0]
    x1, k1, vt1, q1, qm1, sg1 = _mid(
        y0, qm0, sg0, x, km, vmh, _bf16(a_w_out[0]), kv_norm.reshape(1, D),
        _bf16(w_kv_shared[:, :dw]), _bf16(w_kv_shared[:, dw:].T), b_norm[0].reshape(1, D),
        _bf16(wb[:, :dw] * (SCALE * LOG2E)), _bf16(wb[:, dw:dw + mw]), _bf16(wb[:, dw + mw:]),
        gqm1, gmat)
    y1 = _sb_attn(q1, k1, vt1, sg1, tri)
    return _out_b(y1, qm1, sg1, x1, km, vmh, _bf16(b_w_out[0]))
```
